```python
import functools
import jax, jax.numpy as jnp
from jax import lax
import numpy as np

D_MODEL = 2048
BATCH = 2
SEQ = 4096
DEPTH = 2
DEC_BATCH = 128
DEC_SEQ = 4
PAST_LEN = 2048
PAGE_SIZE = 128

N_HEADS = 8
HEAD_DIM = 128
ATTN_WIDTH = N_HEADS * HEAD_DIM
IDX_HEADS = 16
IDX_DIM = 128
TOPK_MAX = 256
Q_BLOCK = 128
CONV_CH = D_MODEL // 2
CONV_WIDTH = 31
CONV_BUF = CONV_WIDTH - 1
D_FF = 4 * D_MODEL
NORM_EPS = 1e-6
LN_EPS = 1e-5
IN_SIZES = (ATTN_WIDTH, ATTN_WIDTH, ATTN_WIDTH, IDX_HEADS * IDX_DIM, IDX_DIM, IDX_HEADS, 2 * CONV_CH, 2 * D_MODEL)
IN_WIDTH = sum(IN_SIZES)

kernel_name = "gated_dsa_conformer_hybrid_step"


def rmsnorm(x, g):
    xf = x.astype(jnp.float32)
    y = xf * lax.rsqrt(jnp.mean(xf * xf, axis=-1, keepdims=True) + NORM_EPS)
    return (y * g.astype(jnp.float32)).astype(x.dtype)


def layernorm(x, g, b):
    xf = x.astype(jnp.float32)
    mu = jnp.mean(xf, axis=-1, keepdims=True)
    xc = xf - mu
    y = xc * lax.rsqrt(jnp.mean(xc * xc, axis=-1, keepdims=True) + LN_EPS)
    return (y * g.astype(jnp.float32) + b.astype(jnp.float32)).astype(x.dtype)


def split_in(p):
    B, T = p.shape[:2]
    offs = np.cumsum(IN_SIZES)[:-1].tolist()
    q, k, v, iq, ik, iw, glu, gates = jnp.split(p, offs, axis=-1)
    q = q.reshape(B, T, N_HEADS, HEAD_DIM)
    k = k.reshape(B, T, N_HEADS, HEAD_DIM)
    v = v.reshape(B, T, N_HEADS, HEAD_DIM)
    iq = iq.reshape(B, T, IDX_HEADS, IDX_DIM)
    return q, k, v, iq, ik, iw, glu, gates


def indexer_scores(iq, ik, iw):
    s = jnp.einsum('bqhd,bld->bqhl', iq.astype(jnp.float32), ik.astype(jnp.float32)) * (IDX_DIM ** -0.5)
    return jnp.einsum('bqhl,bqh->bql', jax.nn.relu(s), iw.astype(jnp.float32) * (IDX_HEADS ** -0.5))


def dsa_prompt(q, k, v, iq, ik, iw):
    B, S = q.shape[:2]
    n_blk = S // Q_BLOCK
    k_sel = min(TOPK_MAX, S // 4)
    key_pos = jnp.arange(S)

    def to_blocks(a):
        return a.reshape((B, n_blk, Q_BLOCK) + a.shape[2:]).swapaxes(0, 1)

    def block(args):
        qb, iqb, iwb, t0 = args
        t = t0 + jnp.arange(Q_BLOCK)
        causal = key_pos[None, :] <= t[:, None]
        score = jnp.where(causal[None], indexer_scores(iqb, ik, iwb), -jnp.inf)
        _, idx = lax.top_k(score, k_sel)
        valid = idx <= t[None, :, None]
        kg = jax.vmap(lambda a, i: a[i])(k, idx)
        vg = jax.vmap(lambda a, i: a[i])(v, idx)
        logits = jnp.einsum('bqhd,bqkhd->bqhk', qb, kg).astype(jnp.float32) * (HEAD_DIM ** -0.5)
        logits = jnp.where(valid[:, :, None, :], logits, -jnp.inf)
        p = jax.nn.softmax(logits, axis=-1).astype(vg.dtype)
        return jnp.einsum('bqhk,bqkhd->bqhd', p, vg)

    starts = jnp.arange(n_blk, dtype=jnp.int32) * Q_BLOCK
    out = lax.map(block, (to_blocks(q), to_blocks(iq), to_blocks(iw), starts))
    return out.swapaxes(0, 1).reshape(B, S, ATTN_WIDTH)


def dsa_sample(q, k, v, iq, ik, iw, cache_k, cache_v, cache_idx_k, page_table, layer):
    DB, T = q.shape[:2]
    n_pages = page_table.shape[1]
    past = n_pages * PAGE_SIZE
    L = past + T
    k_sel = min(TOPK_MAX, L // 4)
    ik_past = cache_idx_k[layer, page_table].reshape(DB, past, IDX_DIM).astype(ik.dtype)
    ik_all = jnp.concatenate([ik_past, ik], axis=1)
    t = past + jnp.arange(T)
    causal = jnp.arange(L)[None, :] <= t[:, None]
    score = jnp.where(causal[None], indexer_scores(iq, ik_all, iw), -jnp.inf)
    _, idx = lax.top_k(score, k_sel)
    from_past = idx < past
    pidx = jnp.minimum(idx, past - 1)
    phys = jax.vmap(lambda pt, i: pt[i])(page_table, pidx // PAGE_SIZE)
    off = pidx % PAGE_SIZE
    kg = cache_k[layer, phys, off].astype(q.dtype)
    vg = cache_v[layer, phys, off].astype(v.dtype)
    new_pos = past + jnp.arange(T)
    sel_new = jnp.any(idx[..., None] == new_pos, axis=2) & (new_pos[None, :] <= t[:, None])[None]
    scale = HEAD_DIM ** -0.5
    lp = jnp.einsum('bqhd,bqkhd->bqhk', q, kg).astype(jnp.float32) * scale
    lp = jnp.where(from_past[:, :, None, :], lp, -jnp.inf)
    ln = jnp.einsum('bqhd,bshd->bqhs', q, k).astype(jnp.float32) * scale
    ln = jnp.where(sel_new[:, :, None, :], ln, -jnp.inf)
    p = jax.nn.softmax(jnp.concatenate([lp, ln], axis=-1), axis=-1).astype(v.dtype)
    out = (jnp.einsum('bqhk,bqkhd->bqhd', p[..., :k_sel], vg)
           + jnp.einsum('bqhs,bshd->bqhd', p[..., k_sel:], v))
    return out.reshape(DB, T, ATTN_WIDTH)


def conv_module(glu_in, buf, w_dw, b_dw, ln_g, ln_b, w_out):
    a, g = jnp.split(glu_in, 2, axis=-1)
    u = a * jax.nn.sigmoid(g)
    full = jnp.concatenate([buf.astype(u.dtype), u], axis=1)
    y = lax.conv_general_dilated(full, w_dw[:, None, :].astype(full.dtype), (1,), 'VALID',
                                 dimension_numbers=('NWC', 'WIO', 'NWC'),
                                 feature_group_count=CONV_CH) + b_dw
    y = jax.nn.silu(layernorm(y, ln_g, ln_b))
    return y @ w_out, full[:, -CONV_BUF:]


def setup_inputs(seed: int = 0) -> dict:
    key = jax.random.key(seed)
    ks = jax.random.split(key, 24)
    n_pages = PAST_LEN // PAGE_SIZE
    n_used = DEC_BATCH * n_pages
    n_pool = (5 * n_used) // 4
    f32 = jnp.float32

    def nrm(k, shape, scale=1.0):
        return jax.random.normal(k, shape, f32) * scale

    def gain(k, shape):
        return 1.0 + 0.05 * jax.random.normal(k, shape, f32)

    perm = jax.random.permutation(ks[0], n_pool)[:n_used]
    page_table = perm.reshape(DEC_BATCH, n_pages).astype(jnp.int32)
    return {
        "x_prompt": nrm(ks[1], (BATCH, SEQ, D_MODEL)),
        "x_sample": nrm(ks[2], (DEC_BATCH, DEC_SEQ, D_MODEL)),
        "cache_k": nrm(ks[3], (DEPTH, n_pool, PAGE_SIZE, N_HEADS, HEAD_DIM)),
        "cache_v": nrm(ks[4], (DEPTH, n_pool, PAGE_SIZE, N_HEADS, HEAD_DIM)),
        "cache_idx_k": nrm(ks[5], (DEPTH, n_pool, PAGE_SIZE, IDX_DIM)),
        "state_conv": nrm(ks[6], (DEPTH, DEC_BATCH, CONV_BUF, CONV_CH), 0.5),
        "page_table": page_table,
        "w_in": nrm(ks[7], (DEPTH, D_MODEL, IN_WIDTH), D_MODEL ** -0.5),
        "w_attn_up": nrm(ks[8], (DEPTH, ATTN_WIDTH, D_MODEL), ATTN_WIDTH ** -0.5),
        "w_conv_dw": nrm(ks[9], (DEPTH, CONV_WIDTH, CONV_CH), CONV_WIDTH ** -0.5),
        "b_conv_dw": nrm(ks[10], (DEPTH, CONV_CH), 0.02),
        "g_conv_ln": gain(ks[11], (DEPTH, CONV_CH)),
        "b_conv_ln": nrm(ks[12], (DEPTH, CONV_CH), 0.02),
        "w_conv_out": nrm(ks[13], (DEPTH, CONV_CH, D_MODEL), CONV_CH ** -0.5),
        "w_o": nrm(ks[14], (DEPTH, D_MODEL, D_MODEL), D_MODEL ** -0.5),
        "w_ffn_up": nrm(ks[15], (DEPTH, D_MODEL, D_FF), D_MODEL ** -0.5),
        "w_ffn_down": nrm(ks[16], (DEPTH, D_FF, D_MODEL), D_FF ** -0.5),
        "g_pre_mix": gain(ks[17], (DEPTH, D_MODEL)),
        "g_post_mix": gain(ks[18], (DEPTH, D_MODEL)),
        "g_pre_ffn": gain(ks[19], (DEPTH, D_MODEL)),
        "g_post_ffn": gain(ks[20], (DEPTH, D_MODEL)),
    }


def reference(x_prompt, x_sample, cache_k, cache_v, cache_idx_k, state_conv, page_table,
              w_in, w_attn_up, w_conv_dw, b_conv_dw, g_conv_ln, b_conv_ln, w_conv_out, w_o,
              w_ffn_up, w_ffn_down, g_pre_mix, g_post_mix, g_pre_ffn, g_post_ffn):

    def layer(x, l, conv_buf, attend):
        h = rmsnorm(x, g_pre_mix[l])
        q, k, v, iq, ik, iw, glu, gates = split_in(h @ w_in[l])
        a = attend(q, k, v, iq, ik, iw) @ w_attn_up[l]
        c, new_buf = conv_module(glu, conv_buf, w_conv_dw[l], b_conv_dw[l],
                                 g_conv_ln[l], b_conv_ln[l], w_conv_out[l])
        g_a, g_c = jnp.split(jax.nn.sigmoid(gates), 2, axis=-1)
        mixed = (g_a * a + g_c * c) @ w_o[l]
        x = x + rmsnorm(mixed, g_post_mix[l])
        f = rmsnorm(x, g_pre_ffn[l])
        f = jnp.square(jax.nn.relu(f @ w_ffn_up[l])) @ w_ffn_down[l]
        x = x + rmsnorm(f, g_post_ffn[l])
        return x, k, v, ik, new_buf

    pk, pv, pik, pconv = [], [], [], []
    sk, sv, sik, sconv = [], [], [], []
    xp, xs = x_prompt, x_sample
    for l in range(DEPTH):
        zero_buf = jnp.zeros((xp.shape[0], CONV_BUF, CONV_CH), xp.dtype)
        xp, k_, v_, ik_, buf_ = layer(xp, l, zero_buf, dsa_prompt)
        pk.append(k_); pv.append(v_); pik.append(ik_); pconv.append(buf_)
        attend_s = functools.partial(dsa_sample, cache_k=cache_k, cache_v=cache_v,
                                     cache_idx_k=cache_idx_k, page_table=page_table, layer=l)
        xs, k_, v_, ik_, buf_ = layer(xs, l, state_conv[l], attend_s)
        sk.append(k_); sv.append(v_); sik.append(ik_); sconv.append(buf_)

    return (xp, xs, jnp.stack(pk), jnp.stack(pv), jnp.stack(pik), jnp.stack(pconv),
            jnp.stack(sk), jnp.stack(sv), jnp.stack(sik), jnp.stack(sconv))
```

```python
import functools

import jax
import jax.numpy as jnp
import numpy as np
from jax import lax
from jax.experimental import pallas as pl
from jax.experimental.pallas import tpu as pltpu

N_HEADS = 8
HEAD_DIM = 128
IDX_HEADS = 16
IDX_DIM = 128
TOPK_MAX = 256
CONV_WIDTH = 31
CONV_BUF = CONV_WIDTH - 1
PAGE_SIZE = 128
NORM_EPS = 1e-6
LN_EPS = 1e-5

LANES = 128
SUBLANES = 8
VMEM_LIMIT = 56 * 1024 * 1024

INT_MIN = np.int32(-2 ** 31)
BF16 = jnp.bfloat16
F32 = jnp.float32


def _cparams(*sem):
    return pltpu.CompilerParams(dimension_semantics=sem, vmem_limit_bytes=VMEM_LIMIT)


def _rms(x, g):
    ms = jnp.mean(x * x, axis=-1, keepdims=True)
    return x * lax.rsqrt(ms + NORM_EPS) * g


def _nt_dot(a, b):
    return lax.dot_general(a, b, (((1,), (1,)), ((), ())), preferred_element_type=F32)


def _nn_dot(a, b):
    return jnp.dot(a, b, preferred_element_type=F32)


def _order_key(x):
    bits = pltpu.bitcast(x, jnp.int32)
    return bits ^ ((bits >> 31) & jnp.int32(0x7FFFFFFF))


_TN = 512
_SEGS = (("q", 2), ("k", 2), ("v", 2), ("iq", 4), ("glu", 4), ("gates", 8), ("small", 1))
_SEG_START = {}
_acc = 0
for _n, _c in _SEGS:
    _SEG_START[_n] = _acc
    _acc += _c
_N_COL_TILES = _acc
_SEG_COUNT = dict(_SEGS)


def _pack_w_in(w_in):
    aw = N_HEADS * HEAD_DIM
    d_model = w_in.shape[1]
    conv_ch = d_model // 2
    sizes = (aw, aw, aw, IDX_HEADS * IDX_DIM, IDX_DIM, IDX_HEADS, 2 * conv_ch, 2 * d_model)
    offs = np.concatenate([[0], np.cumsum(sizes)])
    q, k, v, iq, ik, iw, glu, gates = [w_in[:, :, offs[i]:offs[i + 1]] for i in range(8)]
    pad = jnp.zeros(w_in.shape[:2] + (_TN - IDX_DIM - IDX_HEADS,), w_in.dtype)
    return jnp.concatenate([q, k, v, iq, glu, gates, ik, iw, pad], axis=-1).astype(BF16)


def _inproj_kernel(x_ref, g_ref, w_ref, q_ref, k_ref, kb_ref, v_ref, vb_ref, iq_ref, glu_ref, gates_ref,
                   ik_ref, ikb_ref, iw_ref, h_scr):
    j = pl.program_id(1)

    @pl.when(j == 0)
    def _():
        h_scr[...] = _rms(x_ref[...], g_ref[...]).astype(BF16)

    def seg(name):
        s = _SEG_START[name]
        return (j >= s) & (j < s + _SEG_COUNT[name])

    def proj():
        return _nn_dot(h_scr[...], w_ref[...])

    @pl.when(seg("q"))
    def _():
        q_ref[...] = proj().astype(BF16)

    @pl.when(seg("k"))
    def _():
        r = proj()
        k_ref[...] = r
        kb_ref[...] = r.astype(BF16)

    @pl.when(seg("v"))
    def _():
        r = proj()
        v_ref[...] = r
        vb_ref[...] = r.astype(BF16)

    @pl.when(seg("iq"))
    def _():
        iq_ref[...] = proj().astype(BF16)

    @pl.when(seg("glu"))
    def _():
        glu_ref[...] = proj()

    @pl.when(seg("gates"))
    def _():
        gates_ref[...] = proj()

    @pl.when(seg("small"))
    def _():
        r = proj()
        ik_ref[...] = r[:, :IDX_DIM]
        ikb_ref[...] = r[:, :IDX_DIM].astype(BF16)
        iw_ref[...] = r[:, IDX_DIM:2 * IDX_DIM]


def _in_projection(x2d, g, w_packed, tm):
    m, d = x2d.shape
    aw = N_HEADS * HEAD_DIM

    def col_map(name):
        s, c = _SEG_START[name], _SEG_COUNT[name]
        return lambda i, j: (i, jnp.clip(j - s, 0, c - 1))

    def out(name, width_total, dtype, block_w=_TN):
        return jax.ShapeDtypeStruct((m, width_total), dtype), pl.BlockSpec((tm, block_w), col_map(name))

    outs = [
        out("q", aw, BF16), out("k", aw, F32), out("k", aw, BF16), out("v", aw, F32), out("v", aw, BF16),
        out("iq", IDX_HEADS * IDX_DIM, BF16), out("glu", d, F32), out("gates", 2 * d, F32),
        out("small", IDX_DIM, F32, IDX_DIM), out("small", IDX_DIM, BF16, IDX_DIM),
        out("small", LANES, F32, LANES),
    ]
    res = pl.pallas_call(
        _inproj_kernel,
        grid=(m // tm, _N_COL_TILES),
        in_specs=[pl.BlockSpec((tm, d), lambda i, j: (i, 0)),
                  pl.BlockSpec((1, d), lambda i, j: (0, 0)),
                  pl.BlockSpec((d, _TN), lambda i, j: (0, j))],
        out_specs=[o[1] for o in outs],
        out_shape=[o[0] for o in outs],
        scratch_shapes=[pltpu.VMEM((tm, d), BF16)],
        compiler_params=_cparams("parallel", "arbitrary"),
        name="in_projection",
    )(x2d, g.reshape(1, d), w_packed)
    names = ("q", "k", "kb", "v", "vb", "iq", "glu", "gates", "ik", "ikb", "iw")
    return dict(zip(names, res))


_TAIL = 32


def _ln_swish(y, g, b):
    mu = jnp.mean(y, axis=-1, keepdims=True)
    yc = y - mu
    var = jnp.mean(yc * yc, axis=-1, keepdims=True)
    z = yc * lax.rsqrt(var + LN_EPS) * g + b
    return z * jax.nn.sigmoid(z)


def _conv_prompt_kernel(glu_ref, w_ref, bdw_ref, lng_ref, lnb_ref, y_ref, buf_ref, ext_scr, acc_scr, *, tt, ch):
    i = pl.program_id(1)

    @pl.when(i == 0)
    def _():
        ext_scr[0:_TAIL, :] = jnp.zeros((_TAIL, ch), F32)

    a = glu_ref[:, :ch]
    g = glu_ref[:, ch:]
    ext_scr[_TAIL:, :] = a * jax.nn.sigmoid(g)

    off0 = _TAIL - CONV_BUF
    rows = 64
    for c in range(ch // LANES):
        cs = slice(c * LANES, (c + 1) * LANES)
        wc = w_ref[:, cs]
        for rc in range(tt // rows):
            acc = jnp.zeros((rows, LANES), F32)
            for j in range(CONV_WIDTH):
                acc = acc + ext_scr[rc * rows + off0 + j: rc * rows + off0 + j + rows, cs] * wc[j:j + 1, :]
            acc_scr[rc * rows:(rc + 1) * rows, cs] = acc

    y = acc_scr[...] + bdw_ref[...]
    y_ref[...] = _ln_swish(y, lng_ref[...], lnb_ref[...]).astype(y_ref.dtype)

    @pl.when(i == pl.num_programs(1) - 1)
    def _():
        buf_ref[...] = ext_scr[tt + _TAIL - CONV_BUF: tt + _TAIL, :]

    ext_scr[0:_TAIL, :] = ext_scr[tt:tt + _TAIL, :]


def _conv_prompt(glu2d, w_dw, b_dw, ln_g, ln_b, batch, seq, tt=256):
    ch = w_dw.shape[1]
    nt = seq // tt
    row = lambda a: a.reshape(1, ch)
    vec = pl.BlockSpec((1, ch), lambda b, i: (0, 0))
    y, buf = pl.pallas_call(
        functools.partial(_conv_prompt_kernel, tt=tt, ch=ch),
        grid=(batch, nt),
        in_specs=[pl.BlockSpec((tt, 2 * ch), lambda b, i: (b * nt + i, 0)),
                  pl.BlockSpec((CONV_WIDTH, ch), lambda b, i: (0, 0)), vec, vec, vec],
        out_specs=[pl.BlockSpec((tt, ch), lambda b, i: (b * nt + i, 0)),
                   pl.BlockSpec((None, CONV_BUF, ch), lambda b, i: (b, 0, 0))],
        out_shape=[jax.ShapeDtypeStruct((batch * seq, ch), BF16),
                   jax.ShapeDtypeStruct((batch, CONV_BUF, ch), F32)],
        scratch_shapes=[pltpu.VMEM((tt + _TAIL, ch), F32), pltpu.VMEM((tt, ch), F32)],
        compiler_params=_cparams("parallel", "arbitrary"),
        name="conv_prompt",
    )(glu2d, w_dw, row(b_dw), row(ln_g), row(ln_b))
    return y, buf


def _conv_sample_kernel(glu_ref, st_ref, w_ref, bdw_ref, lng_ref, lnb_ref, y_ref, buf_ref, full_scr, *, t_new, ch):
    g_sz = st_ref.shape[0]
    a = glu_ref[:, :, :ch]
    g = glu_ref[:, :, ch:]
    full_scr[:, 0:CONV_BUF, :] = st_ref[...]
    full_scr[:, CONV_BUF:CONV_BUF + t_new, :] = a * jax.nn.sigmoid(g)
    pad0 = CONV_BUF + t_new
    full_scr[:, pad0:, :] = jnp.zeros((g_sz, full_scr.shape[1] - pad0, ch), F32)
    w = w_ref[...]
    for t in range(t_new):
        y = jnp.sum(full_scr[:, t:t + CONV_WIDTH + 1, :] * w[None], axis=1) + bdw_ref[...]
        y_ref[:, t, :] = _ln_swish(y, lng_ref[...], lnb_ref[...]).astype(y_ref.dtype)
    buf_ref[...] = full_scr[:, t_new:t_new + CONV_BUF, :]


def _conv_sample(glu3d, state, w_dw, b_dw, ln_g, ln_b, g_sz=16):
    db, t_new, _ = glu3d.shape
    ch = w_dw.shape[1]
    row = lambda a: a.reshape(1, ch)
    vec = pl.BlockSpec((1, ch), lambda b: (0, 0))
    w_pad = jnp.concatenate([w_dw, jnp.zeros((1, ch), w_dw.dtype)], axis=0)
    y, buf = pl.pallas_call(
        functools.partial(_conv_sample_kernel, t_new=t_new, ch=ch),
        grid=(db // g_sz,),
        in_specs=[pl.BlockSpec((g_sz, t_new, 2 * ch), lambda b: (b, 0, 0)),
                  pl.BlockSpec((g_sz, CONV_BUF, ch), lambda b: (b, 0, 0)),
                  pl.BlockSpec((CONV_WIDTH + 1, ch), lambda b: (0, 0)), vec, vec, vec],
        out_specs=[pl.BlockSpec((g_sz, t_new, ch), lambda b: (b, 0, 0)),
                   pl.BlockSpec((g_sz, CONV_BUF, ch), lambda b: (b, 0, 0))],
        out_shape=[jax.ShapeDtypeStruct((db, t_new, ch), BF16),
                   jax.ShapeDtypeStruct((db, CONV_BUF, ch), F32)],
        scratch_shapes=[pltpu.VMEM((g_sz, 40, ch), F32)],
        compiler_params=_cparams("parallel"),
        name="conv_sample",
    )(glu3d, state, w_pad, row(b_dw), row(ln_g), row(ln_b))
    return y, buf


def _kth_largest_rows(count_ge, shape, k_sel):
    def bit_step(b, r):
        trial = r + lax.shift_left(jnp.int32(1), jnp.int32(31) - b)
        return jnp.where(count_ge(trial) >= k_sel, trial, r)
    r = lax.fori_loop(0, 32, bit_step, jnp.full(shape, INT_MIN, jnp.int32))
    return jnp.maximum(r, INT_MIN + 1)


def _dsa_prompt_kernel(q_ref, iq_ref, iw_ref, ik_ref, k_ref, vt_ref, o_ref, key_scr, logit_scr, *, tq, k_sel):
    i = pl.program_id(1)
    n_chunks = i + 1
    q0 = i * tq
    sub = LANES
    w_rows = iw_ref[...].T * (IDX_DIM ** -0.5 * IDX_HEADS ** -0.5)

    def score_chunk(c, carry):
        for s_i in range(tq // sub):
            r0 = pl.multiple_of(c * tq + s_i * sub, sub)
            ikc = ik_ref[pl.ds(r0, sub), :]
            acc = jnp.zeros((sub, tq), F32)
            for h in range(IDX_HEADS):
                s = _nt_dot(ikc, iq_ref[:, h * IDX_DIM:(h + 1) * IDX_DIM])
                acc = acc + jnp.maximum(s, 0.0) * w_rows[h:h + 1, :]
            kidx = r0 + lax.broadcasted_iota(jnp.int32, (sub, tq), 0)
            qidx = q0 + lax.broadcasted_iota(jnp.int32, (sub, tq), 1)
            key_scr[pl.ds(r0, sub), :] = jnp.where(kidx <= qidx, _order_key(acc), INT_MIN)
        return carry

    lax.fori_loop(0, n_chunks, score_chunk, 0)

    def count_ge(trial):
        def cnt_chunk(c, cnt):
            r0 = pl.multiple_of(c * tq, tq)
            m = jnp.where(key_scr[pl.ds(r0, tq), :] >= trial, 1, 0)
            return cnt + m.reshape(tq // SUBLANES, SUBLANES, tq).sum(axis=0)
        cnt = lax.fori_loop(0, n_chunks, cnt_chunk, jnp.zeros((SUBLANES, tq), jnp.int32))
        return cnt.sum(axis=0, keepdims=True)

    thr = _kth_largest_rows(count_ge, (1, tq), k_sel)

    scale = HEAD_DIM ** -0.5

    def head_body(h, carry):
        c0 = pl.multiple_of(h * HEAD_DIM, HEAD_DIM)
        qh = q_ref[:, pl.ds(c0, HEAD_DIM)]

        def pass1(c, m8):
            r0 = pl.multiple_of(c * tq, tq)
            lg = _nt_dot(k_ref[pl.ds(r0, tq), pl.ds(c0, HEAD_DIM)], qh) * scale
            lg = jnp.where(key_scr[pl.ds(r0, tq), :] >= thr, lg, -jnp.inf)
            logit_scr[pl.ds(r0, tq), :] = lg
            return jnp.maximum(m8, lg.reshape(tq // SUBLANES, SUBLANES, tq).max(axis=0))

        m8 = lax.fori_loop(0, n_chunks, pass1, jnp.full((SUBLANES, tq), -jnp.inf, F32))
        m = m8.max(axis=0, keepdims=True)

        def pass2(c, carry2):
            l8, acc = carry2
            r0 = pl.multiple_of(c * tq, tq)
            p = jnp.exp(logit_scr[pl.ds(r0, tq), :] - m)
            l8 = l8 + p.reshape(tq // SUBLANES, SUBLANES, tq).sum(axis=0)
            acc = acc + _nn_dot(vt_ref[pl.ds(c0, HEAD_DIM), pl.ds(r0, tq)], p.astype(BF16))
            return l8, acc

        l8, acc = lax.fori_loop(0, n_chunks, pass2,
                                (jnp.zeros((SUBLANES, tq), F32), jnp.zeros((HEAD_DIM, tq), F32)))
        out_t = acc * (1.0 / l8.sum(axis=0, keepdims=True))
        o_ref[:, pl.ds(c0, HEAD_DIM)] = out_t.T.astype(o_ref.dtype)
        return carry

    lax.fori_loop(0, N_HEADS, head_body, 0)


def _dsa_prompt(q, iq, iw, ikb, kb, vt, batch, seq, tq=256):
    nq = seq // tq
    aw = N_HEADS * HEAD_DIM
    k_sel = min(TOPK_MAX, seq // 4)
    blk = lambda w: pl.BlockSpec((tq, w), lambda b, i: (b * nq + i, 0))
    return pl.pallas_call(
        functools.partial(_dsa_prompt_kernel, tq=tq, k_sel=k_sel),
        grid=(batch, nq),
        in_specs=[blk(aw), blk(IDX_HEADS * IDX_DIM), blk(LANES),
                  pl.BlockSpec((seq, IDX_DIM), lambda b, i: (b, 0)),
                  pl.BlockSpec((seq, aw), lambda b, i: (b, 0)),
                  pl.BlockSpec((None, aw, seq), lambda b, i: (b, 0, 0))],
        out_specs=blk(aw),
        out_shape=jax.ShapeDtypeStruct((batch * seq, aw), BF16),
        scratch_shapes=[pltpu.VMEM((seq, tq), jnp.int32), pltpu.VMEM((seq, tq), F32)],
        compiler_params=_cparams("parallel", "arbitrary"),
        name="dsa_prompt",
    )(q, iq, iw, ikb, kb, vt)


def _s_scores_kernel(pt_ref, iq_ref, iwc_ref, *rest, n_pages, t_new):
    pages, ikn_ref, o_ref = rest[:n_pages], rest[n_pages], rest[n_pages + 1]
    iq = iq_ref[...]
    wc = iwc_ref[...] * (IDX_DIM ** -0.5 * IDX_HEADS ** -0.5)
    o_ref[...] = jnp.full(o_ref.shape, -jnp.inf, F32)
    for j in range(n_pages + 1):
        ikp = pages[j][...].astype(BF16) if j < n_pages else ikn_ref[...]
        z = jnp.maximum(_nt_dot(iq, ikp), 0.0) * wc
        sc = z.reshape(t_new, IDX_HEADS, PAGE_SIZE).sum(axis=1)
        if j == n_pages:
            key_i = lax.broadcasted_iota(jnp.int32, sc.shape, 1)
            t_i = lax.broadcasted_iota(jnp.int32, sc.shape, 0)
            sc = jnp.where(key_i <= t_i, sc, -jnp.inf)
        o_ref[0:t_new, j * PAGE_SIZE:(j + 1) * PAGE_SIZE] = sc


def _s_scores(page_table, iq64, iwcol, cache_ik_l, ikn_pad, t_new):
    db, n_pages = page_table.shape
    rows = t_new * IDX_HEADS
    width = (n_pages + 1) * PAGE_SIZE

    def page_spec(j):
        return pl.BlockSpec((None, PAGE_SIZE, IDX_DIM), lambda b, pt: (pt[b, j], 0, 0))

    grid_spec = pltpu.PrefetchScalarGridSpec(
        num_scalar_prefetch=1,
        grid=(db,),
        in_specs=[pl.BlockSpec((rows, IDX_DIM), lambda b, pt: (b, 0)),
                  pl.BlockSpec((rows, 1), lambda b, pt: (b, 0))]
                 + [page_spec(j) for j in range(n_pages)]
                 + [pl.BlockSpec((None, PAGE_SIZE, IDX_DIM), lambda b, pt: (b, 0, 0))],
        out_specs=pl.BlockSpec((None, SUBLANES, width), lambda b, pt: (b, 0, 0)),
    )
    return pl.pallas_call(
        functools.partial(_s_scores_kernel, n_pages=n_pages, t_new=t_new),
        grid_spec=grid_spec,
        out_shape=jax.ShapeDtypeStruct((db, SUBLANES, width), F32),
        compiler_params=_cparams("arbitrary"),
        name="sample_scores",
    )(page_table, iq64, iwcol, *([cache_ik_l] * n_pages), ikn_pad)


def _s_select_kernel(s_ref, o_ref, key_scr, *, k_sel):
    s = s_ref[...]
    key_scr[...] = jnp.where(s == -jnp.inf, INT_MIN, _order_key(s))

    def count_ge(trial):
        return jnp.sum(jnp.where(key_scr[...] >= trial, 1.0, 0.0), axis=1, keepdims=True)

    thr = _kth_largest_rows(count_ge, (s.shape[0], 1), k_sel)
    o_ref[...] = jnp.where(key_scr[...] >= thr, 1.0, 0.0)


def _s_select(scores2d, k_sel, rows=256):
    m, width = scores2d.shape
    return pl.pallas_call(
        functools.partial(_s_select_kernel, k_sel=k_sel),
        grid=(m // rows,),
        in_specs=[pl.BlockSpec((rows, width), lambda i: (i, 0))],
        out_specs=pl.BlockSpec((rows, width), lambda i: (i, 0)),
        out_shape=jax.ShapeDtypeStruct((m, width), F32),
        scratch_shapes=[pltpu.VMEM((rows, width), jnp.int32)],
        compiler_params=_cparams("parallel"),
        name="sample_select",
    )(scores2d)


def _s_attn_kernel(pt_ref, q_ref, sel_ref, seln_ref, k_ref, v_ref, kn_ref, vn_ref, o_ref,
                   qbd_scr, m_scr, l_scr, acc_scr):
    j = pl.program_id(1)
    aw = N_HEADS * HEAD_DIM
    rows = N_HEADS * SUBLANES

    @pl.when(j == 0)
    def _():
        q_rep = jnp.concatenate([q_ref[...]] * N_HEADS, axis=0)
        row_h = lax.broadcasted_iota(jnp.int32, (rows, aw), 0) // SUBLANES
        col_h = lax.broadcasted_iota(jnp.int32, (rows, aw), 1) // HEAD_DIM
        qbd_scr[...] = jnp.where(row_h == col_h, q_rep, jnp.zeros_like(q_rep))
        m_scr[...] = jnp.full(m_scr.shape, -jnp.inf, F32)
        l_scr[...] = jnp.zeros(l_scr.shape, F32)
        acc_scr[...] = jnp.zeros(acc_scr.shape, F32)

    def absorb(kp, vp, sel8):
        lg = _nt_dot(qbd_scr[...], kp) * (HEAD_DIM ** -0.5)
        sel = jnp.concatenate([sel8] * N_HEADS, axis=0)
        lg = jnp.where(sel > 0.0, lg, -jnp.inf)
        m_old = m_scr[...]
        m_new = jnp.maximum(m_old, jnp.max(lg, axis=1, keepdims=True))
        m_safe = jnp.where(m_new == -jnp.inf, 0.0, m_new)
        p = jnp.exp(lg - m_safe)
        alpha = jnp.exp(m_old - m_safe)
        l_scr[...] = alpha * l_scr[...] + jnp.sum(p, axis=1, keepdims=True)
        acc_scr[...] = alpha * acc_scr[...] + _nn_dot(p.astype(BF16), vp)
        m_scr[...] = m_new

    absorb(k_ref[...].astype(BF16), v_ref[...].astype(BF16), sel_ref[...])

    @pl.when(j == pl.num_programs(1) - 1)
    def _():
        zpad = jnp.zeros((PAGE_SIZE - SUBLANES, aw), BF16)
        absorb(jnp.concatenate([kn_ref[...], zpad], axis=0), jnp.concatenate([vn_ref[...], zpad], axis=0),
               seln_ref[...])
        l = l_scr[...]
        out = acc_scr[...] / jnp.where(l == 0.0, 1.0, l)
        for h in range(N_HEADS):
            o_ref[:, h * HEAD_DIM:(h + 1) * HEAD_DIM] = out[h * SUBLANES:(h + 1) * SUBLANES,
                                                            h * HEAD_DIM:(h + 1) * HEAD_DIM].astype(o_ref.dtype)


def _s_attn(page_table, q8, sel, cache_k_l, cache_v_l, kn8, vn8):
    db, n_pages = page_table.shape
    aw = N_HEADS * HEAD_DIM
    rows = N_HEADS * SUBLANES
    row_blk = pl.BlockSpec((None, SUBLANES, aw), lambda b, j, pt: (b, 0, 0))
    page_blk = pl.BlockSpec((None, PAGE_SIZE, aw), lambda b, j, pt: (pt[b, j], 0, 0))
    grid_spec = pltpu.PrefetchScalarGridSpec(
        num_scalar_prefetch=1,
        grid=(db, n_pages),
        in_specs=[row_blk,
                  pl.BlockSpec((None, SUBLANES, PAGE_SIZE), lambda b, j, pt: (b, 0, j)),
                  pl.BlockSpec((None, SUBLANES, PAGE_SIZE), lambda b, j, pt: (b, 0, n_pages)),
                  page_blk, page_blk, row_blk, row_blk],
        out_specs=row_blk,
        scratch_shapes=[pltpu.VMEM((rows, aw), BF16), pltpu.VMEM((rows, 1), F32), pltpu.VMEM((rows, 1), F32),
                        pltpu.VMEM((rows, aw), F32)],
    )
    return pl.pallas_call(
        _s_attn_kernel,
        grid_spec=grid_spec,
        out_shape=jax.ShapeDtypeStruct((db, SUBLANES, aw), BF16),
        compiler_params=_cparams("parallel", "arbitrary"),
        name="sample_attention",
    )(page_table, q8, sel, sel, cache_k_l, cache_v_l, kn8, vn8)


def _merge_kernel(attn_ref, y_ref, gates_ref, x_ref, wa_ref, wc_ref, wo_ref, g_ref, o_ref):
    d = x_ref.shape[1]
    a = _nn_dot(attn_ref[...], wa_ref[...])
    c = _nn_dot(y_ref[...], wc_ref[...])
    mixed = jax.nn.sigmoid(gates_ref[:, :d]) * a + jax.nn.sigmoid(gates_ref[:, d:]) * c
    o = _nn_dot(mixed.astype(BF16), wo_ref[...])
    o_ref[...] = x_ref[...] + _rms(o, g_ref[...])


def _merge(attn, y, gates, x2d, wa, wc, wo, g, tm=256):
    m, d = x2d.shape
    aw, ch = attn.shape[1], y.shape[1]
    full = lambda r, c: pl.BlockSpec((r, c), lambda i: (0, 0))
    blk = lambda w: pl.BlockSpec((tm, w), lambda i: (i, 0))
    return pl.pallas_call(
        _merge_kernel,
        grid=(m // tm,),
        in_specs=[blk(aw), blk(ch), blk(2 * d), blk(d), full(aw, d), full(ch, d), full(d, d), full(1, d)],
        out_specs=blk(d),
        out_shape=jax.ShapeDtypeStruct((m, d), F32),
        compiler_params=_cparams("parallel"),
        name="merge",
    )(attn, y, gates, x2d, wa, wc, wo, g.reshape(1, d))


def _ffn_kernel(x_ref, gpre_ref, wu_ref, wd_ref, gpost_ref, o_ref, f_scr, acc_scr):
    f = pl.program_id(1)

    @pl.when(f == 0)
    def _():
        f_scr[...] = _rms(x_ref[...], gpre_ref[...]).astype(BF16)
        acc_scr[...] = jnp.zeros(acc_scr.shape, F32)

    hdn = jnp.square(jnp.maximum(_nn_dot(f_scr[...], wu_ref[...]), 0.0)).astype(BF16)
    acc_scr[...] += _nn_dot(hdn, wd_ref[...])

    @pl.when(f == pl.num_programs(1) - 1)
    def _():
        o_ref[...] = x_ref[...] + _rms(acc_scr[...], gpost_ref[...])


def _ffn(x2d, g_pre, w_up, w_down, g_post, tm=512, tf=512):
    m, d = x2d.shape
    d_ff = w_up.shape[1]
    vec = pl.BlockSpec((1, d), lambda i, f: (0, 0))
    return pl.pallas_call(
        _ffn_kernel,
        grid=(m // tm, d_ff // tf),
        in_specs=[pl.BlockSpec((tm, d), lambda i, f: (i, 0)), vec,
                  pl.BlockSpec((d, tf), lambda i, f: (0, f)),
                  pl.BlockSpec((tf, d), lambda i, f: (f, 0)), vec],
        out_specs=pl.BlockSpec((tm, d), lambda i, f: (i, 0)),
        out_shape=jax.ShapeDtypeStruct((m, d), F32),
        scratch_shapes=[pltpu.VMEM((tm, d), BF16), pltpu.VMEM((tm, d), F32)],
        compiler_params=_cparams("parallel", "arbitrary"),
        name="ffn",
    )(x2d, g_pre.reshape(1, d), w_up, w_down, g_post.reshape(1, d))


def kernel(x_prompt, x_sample, cache_k, cache_v, cache_idx_k, state_conv, page_table, w_in, w_attn_up, w_conv_dw,
           b_conv_dw, g_conv_ln, b_conv_ln, w_conv_out, w_o, w_ffn_up, w_ffn_down, g_pre_mix, g_post_mix,
           g_pre_ffn, g_post_ffn):
    depth = w_in.shape[0]
    batch, seq, d = x_prompt.shape
    db, t_new, _ = x_sample.shape
    n_pool = cache_k.shape[1]
    n_pages = page_table.shape[1]
    aw = N_HEADS * HEAD_DIM
    ch = d // 2
    k_sel_s = min(TOPK_MAX, (n_pages * PAGE_SIZE + t_new) // 4)

    w_in_p = _pack_w_in(w_in)
    wa, wc, wo = w_attn_up.astype(BF16), w_conv_out.astype(BF16), w_o.astype(BF16)
    wu, wd = w_ffn_up.astype(BF16), w_ffn_down.astype(BF16)
    cache_k3 = cache_k.reshape(depth, n_pool, PAGE_SIZE, aw)
    cache_v3 = cache_v.reshape(depth, n_pool, PAGE_SIZE, aw)

    def pad_rows(a2d, rows):
        a3 = a2d.reshape(db, t_new, a2d.shape[1])
        return jnp.pad(a3, ((0, 0), (0, rows - t_new), (0, 0)))

    xp = x_prompt.reshape(batch * seq, d)
    xs = x_sample.reshape(db * t_new, d)
    outs = {n: [] for n in ("pk", "pv", "pik", "pconv", "sk", "sv", "sik", "sconv")}
    for l in range(depth):
        p = _in_projection(xp, g_pre_mix[l], w_in_p[l], tm=512)
        vt = p["vb"].reshape(batch, seq, aw).transpose(0, 2, 1)
        attn = _dsa_prompt(p["q"], p["iq"], p["iw"], p["ikb"], p["kb"], vt, batch, seq)
        y, pbuf = _conv_prompt(p["glu"], w_conv_dw[l], b_conv_dw[l], g_conv_ln[l], b_conv_ln[l], batch, seq)
        xp = _merge(attn, y, p["gates"], xp, wa[l], wc[l], wo[l], g_post_mix[l])
        xp = _ffn(xp, g_pre_ffn[l], wu[l], wd[l], g_post_ffn[l])
        outs["pk"].append(p["k"].reshape(batch, seq, N_HEADS, HEAD_DIM))
        outs["pv"].append(p["v"].reshape(batch, seq, N_HEADS, HEAD_DIM))
        outs["pik"].append(p["ik"].reshape(batch, seq, IDX_DIM))
        outs["pconv"].append(pbuf)

        s = _in_projection(xs, g_pre_mix[l], w_in_p[l], tm=db * t_new)
        iq64 = s["iq"].reshape(db * t_new * IDX_HEADS, IDX_DIM)
        iwcol = s["iw"][:, :IDX_HEADS].reshape(db * t_new * IDX_HEADS, 1)
        ikn_pad = pad_rows(s["ikb"], PAGE_SIZE)
        scores = _s_scores(page_table, iq64, iwcol, cache_idx_k[l], ikn_pad, t_new)
        width = scores.shape[2]
        sel = _s_select(scores.reshape(db * SUBLANES, width), k_sel_s).reshape(db, SUBLANES, width)
        attn_s = _s_attn(page_table, pad_rows(s["q"], SUBLANES), sel, cache_k3[l], cache_v3[l],
                         pad_rows(s["kb"], SUBLANES), pad_rows(s["vb"], SUBLANES))
        attn_s = attn_s[:, :t_new].reshape(db * t_new, aw)
        ys, sbuf = _conv_sample(s["glu"].reshape(db, t_new, 2 * ch), state_conv[l], w_conv_dw[l], b_conv_dw[l],
                                g_conv_ln[l], b_conv_ln[l])
        xs = _merge(attn_s, ys.reshape(db * t_new, ch), s["gates"], xs, wa[l], wc[l], wo[l], g_post_mix[l])
        xs = _ffn(xs, g_pre_ffn[l], wu[l], wd[l], g_post_ffn[l], tm=db * t_new)
        outs["sk"].append(s["k"].reshape(db, t_new, N_HEADS, HEAD_DIM))
        outs["sv"].append(s["v"].reshape(db, t_new, N_HEADS, HEAD_DIM))
        outs["sik"].append(s["ik"].reshape(db, t_new, IDX_DIM))
        outs["sconv"].append(sbuf)

    st = {n: jnp.stack(v) for n, v in outs.items()}
    return (xp.reshape(batch, seq, d), xs.reshape(db, t_new, d), st["pk"], st["pv"], st["pik"], st["pconv"],
            st["sk"], st["sv"], st["sik"], st["sconv"])
```

```python
import functools

import jax
import jax.numpy as jnp
import numpy as np
from jax import lax
from jax.experimental import pallas as pl
from jax.experimental.pallas import tpu as pltpu

N_HEADS = 8
HEAD_DIM = 128
IDX_HEADS = 16
IDX_DIM = 128
TOPK_MAX = 256
CONV_WIDTH = 31
CONV_BUF = CONV_WIDTH - 1
PAGE_SIZE = 128
NORM_EPS = 1e-6
LN_EPS = 1e-5

LANES = 128
SUBLANES = 8
VMEM_LIMIT = 56 * 1024 * 1024

INT_MIN = np.int32(-2 ** 31)
BF16 = jnp.bfloat16
F32 = jnp.float32


def _cparams(*sem):
    return pltpu.CompilerParams(dimension_semantics=sem, vmem_limit_bytes=VMEM_LIMIT)


def _rms(x, g):
    ms = jnp.mean(x * x, axis=-1, keepdims=True)
    return x * lax.rsqrt(ms + NORM_EPS) * g


def _nt_dot(a, b):
    return lax.dot_general(a, b, (((1,), (1,)), ((), ())), preferred_element_type=F32)


def _nn_dot(a, b):
    return jnp.dot(a, b, preferred_element_type=F32)


def _fold_rows(op, x, ways=4):
    parts = [x[r:r + SUBLANES] for r in range(0, x.shape[0], SUBLANES)]
    accs = parts[:ways]
    for i, part in enumerate(parts[ways:]):
        accs[i % ways] = op(accs[i % ways], part)
    while len(accs) > 1:
        accs = [op(a, b) for a, b in zip(accs[0::2], accs[1::2])] + (accs[-1:] if len(accs) % 2 else [])
    return accs[0]


def _order_key(x):
    bits = pltpu.bitcast(x, jnp.int32)
    return bits ^ ((bits >> 31) & jnp.int32(0x7FFFFFFF))


_TN = 512
_SEGS = (("q", 2), ("k", 2), ("v", 2), ("iq", 4), ("glu", 4), ("gates", 8), ("small", 1))
_SEG_START = {}
_acc = 0
for _n, _c in _SEGS:
    _SEG_START[_n] = _acc
    _acc += _c
_N_COL_TILES = _acc
_SEG_COUNT = dict(_SEGS)


def _pack_w_in(w_in):
    aw = N_HEADS * HEAD_DIM
    d_model = w_in.shape[1]
    conv_ch = d_model // 2
    sizes = (aw, aw, aw, IDX_HEADS * IDX_DIM, IDX_DIM, IDX_HEADS, 2 * conv_ch, 2 * d_model)
    offs = np.concatenate([[0], np.cumsum(sizes)])
    q, k, v, iq, ik, iw, glu, gates = [w_in[:, :, offs[i]:offs[i + 1]] for i in range(8)]
    pad = jnp.zeros(w_in.shape[:2] + (_TN - IDX_DIM - IDX_HEADS,), w_in.dtype)
    return jnp.concatenate([q, k, v, iq, glu, gates, ik, iw, pad], axis=-1).astype(BF16)


def _inproj_kernel(x_ref, g_ref, w_ref, q_ref, k_ref, kb_ref, v_ref, vb_ref, iq_ref, glu_ref, gates_ref,
                   ik_ref, ikb_ref, iw_ref, h_scr):
    j = pl.program_id(1)

    @pl.when(j == 0)
    def _():
        h_scr[...] = _rms(x_ref[...], g_ref[...]).astype(BF16)

    def seg(name):
        s = _SEG_START[name]
        return (j >= s) & (j < s + _SEG_COUNT[name])

    def proj():
        return _nn_dot(h_scr[...], w_ref[...])

    @pl.when(seg("q"))
    def _():
        q_ref[...] = proj().astype(BF16)

    @pl.when(seg("k"))
    def _():
        r = proj()
        k_ref[...] = r
        kb_ref[...] = r.astype(BF16)

    @pl.when(seg("v"))
    def _():
        r = proj()
        v_ref[...] = r
        vb_ref[...] = r.astype(BF16)

    @pl.when(seg("iq"))
    def _():
        iq_ref[...] = proj().astype(BF16)

    @pl.when(seg("glu"))
    def _():
        glu_ref[...] = proj()

    @pl.when(seg("gates"))
    def _():
        gates_ref[...] = proj()

    @pl.when(seg("small"))
    def _():
        r = proj()
        ik_ref[...] = r[:, :IDX_DIM]
        ikb_ref[...] = r[:, :IDX_DIM].astype(BF16)
        iw_ref[...] = r[:, IDX_DIM:2 * IDX_DIM]


def _in_projection(x2d, g, w_packed, tm):
    m, d = x2d.shape
    aw = N_HEADS * HEAD_DIM

    def col_map(name):
        s, c = _SEG_START[name], _SEG_COUNT[name]
        return lambda i, j: (i, jnp.clip(j - s, 0, c - 1))

    def out(name, width_total, dtype, block_w=_TN):
        return jax.ShapeDtypeStruct((m, width_total), dtype), pl.BlockSpec((tm, block_w), col_map(name))

    outs = [
        out("q", aw, BF16), out("k", aw, F32), out("k", aw, BF16), out("v", aw, F32), out("v", aw, BF16),
        out("iq", IDX_HEADS * IDX_DIM, BF16), out("glu", d, F32), out("gates", 2 * d, F32),
        out("small", IDX_DIM, F32, IDX_DIM), out("small", IDX_DIM, BF16, IDX_DIM),
        out("small", LANES, F32, LANES),
    ]
    res = pl.pallas_call(
        _inproj_kernel,
        grid=(m // tm, _N_COL_TILES),
        in_specs=[pl.BlockSpec((tm, d), lambda i, j: (i, 0), pipeline_mode=pl.Buffered(1)),
                  pl.BlockSpec((1, d), lambda i, j: (0, 0)),
                  pl.BlockSpec((d, _TN), lambda i, j: (0, j))],
        out_specs=[o[1] for o in outs],
        out_shape=[o[0] for o in outs],
        scratch_shapes=[pltpu.VMEM((tm, d), BF16)],
        compiler_params=_cparams("parallel", "arbitrary"),
        name="in_projection",
    )(x2d, g.reshape(1, d), w_packed)
    names = ("q", "k", "kb", "v", "vb", "iq", "glu", "gates", "ik", "ikb", "iw")
    return dict(zip(names, res))


_TAIL = 32


def _ln_swish(y, g, b):
    mu = jnp.mean(y, axis=-1, keepdims=True)
    yc = y - mu
    var = jnp.mean(yc * yc, axis=-1, keepdims=True)
    z = yc * lax.rsqrt(var + LN_EPS) * g + b
    return z * jax.nn.sigmoid(z)


def _conv_prompt_kernel(glu_ref, w_ref, bdw_ref, lng_ref, lnb_ref, y_ref, buf_ref, ext_scr, acc_scr, *, tt, ch):
    i = pl.program_id(1)

    @pl.when(i == 0)
    def _():
        ext_scr[0:_TAIL, :] = jnp.zeros((_TAIL, ch), F32)

    a = glu_ref[:, :ch]
    g = glu_ref[:, ch:]
    ext_scr[_TAIL:, :] = a * jax.nn.sigmoid(g)

    off0 = _TAIL - CONV_BUF
    rows = 64
    for c in range(ch // LANES):
        cs = slice(c * LANES, (c + 1) * LANES)
        wc = w_ref[:, cs]
        for rc in range(tt // rows):
            acc = jnp.zeros((rows, LANES), F32)
            for j in range(CONV_WIDTH):
                acc = acc + ext_scr[rc * rows + off0 + j: rc * rows + off0 + j + rows, cs] * wc[j:j + 1, :]
            acc_scr[rc * rows:(rc + 1) * rows, cs] = acc

    y = acc_scr[...] + bdw_ref[...]
    y_ref[...] = _ln_swish(y, lng_ref[...], lnb_ref[...]).astype(y_ref.dtype)

    @pl.when(i == pl.num_programs(1) - 1)
    def _():
        buf_ref[...] = ext_scr[tt + _TAIL - CONV_BUF: tt + _TAIL, :]

    ext_scr[0:_TAIL, :] = ext_scr[tt:tt + _TAIL, :]


def _conv_prompt(glu2d, w_dw, b_dw, ln_g, ln_b, batch, seq, tt=256):
    ch = w_dw.shape[1]
    nt = seq // tt
    row = lambda a: a.reshape(1, ch)
    vec = pl.BlockSpec((1, ch), lambda b, i: (0, 0))
    y, buf = pl.pallas_call(
        functools.partial(_conv_prompt_kernel, tt=tt, ch=ch),
        grid=(batch, nt),
        in_specs=[pl.BlockSpec((tt, 2 * ch), lambda b, i: (b * nt + i, 0)),
                  pl.BlockSpec((CONV_WIDTH, ch), lambda b, i: (0, 0)), vec, vec, vec],
        out_specs=[pl.BlockSpec((tt, ch), lambda b, i: (b * nt + i, 0)),
                   pl.BlockSpec((None, CONV_BUF, ch), lambda b, i: (b, 0, 0))],
        out_shape=[jax.ShapeDtypeStruct((batch * seq, ch), BF16),
                   jax.ShapeDtypeStruct((batch, CONV_BUF, ch), F32)],
        scratch_shapes=[pltpu.VMEM((tt + _TAIL, ch), F32), pltpu.VMEM((tt, ch), F32)],
        compiler_params=_cparams("parallel", "arbitrary"),
        name="conv_prompt",
    )(glu2d, w_dw, row(b_dw), row(ln_g), row(ln_b))
    return y, buf


def _conv_sample_kernel(glu_ref, st_ref, w_ref, bdw_ref, lng_ref, lnb_ref, y_ref, buf_ref, full_scr, *, t_new, ch):
    g_sz = st_ref.shape[0]
    a = glu_ref[:, :, :ch]
    g = glu_ref[:, :, ch:]
    full_scr[:, 0:CONV_BUF, :] = st_ref[...]
    full_scr[:, CONV_BUF:CONV_BUF + t_new, :] = a * jax.nn.sigmoid(g)
    pad0 = CONV_BUF + t_new
    full_scr[:, pad0:, :] = jnp.zeros((g_sz, full_scr.shape[1] - pad0, ch), F32)
    w = w_ref[...]
    for t in range(t_new):
        y = jnp.sum(full_scr[:, t:t + CONV_WIDTH + 1, :] * w[None], axis=1) + bdw_ref[...]
        y_ref[:, t, :] = _ln_swish(y, lng_ref[...], lnb_ref[...]).astype(y_ref.dtype)
    buf_ref[...] = full_scr[:, t_new:t_new + CONV_BUF, :]


def _conv_sample(glu3d, state_all, layer, w_dw, b_dw, ln_g, ln_b, g_sz=16):
    db, t_new, _ = glu3d.shape
    ch = w_dw.shape[1]
    row = lambda a: a.reshape(1, ch)
    vec = pl.BlockSpec((1, ch), lambda b: (0, 0))
    w_pad = jnp.concatenate([w_dw, jnp.zeros((1, ch), w_dw.dtype)], axis=0)
    y, buf = pl.pallas_call(
        functools.partial(_conv_sample_kernel, t_new=t_new, ch=ch),
        grid=(db // g_sz,),
        in_specs=[pl.BlockSpec((g_sz, t_new, 2 * ch), lambda b: (b, 0, 0)),
                  pl.BlockSpec((None, g_sz, CONV_BUF, ch), lambda b: (layer, b, 0, 0)),
                  pl.BlockSpec((CONV_WIDTH + 1, ch), lambda b: (0, 0)), vec, vec, vec],
        out_specs=[pl.BlockSpec((g_sz, t_new, ch), lambda b: (b, 0, 0)),
                   pl.BlockSpec((g_sz, CONV_BUF, ch), lambda b: (b, 0, 0))],
        out_shape=[jax.ShapeDtypeStruct((db, t_new, ch), BF16),
                   jax.ShapeDtypeStruct((db, CONV_BUF, ch), F32)],
        scratch_shapes=[pltpu.VMEM((g_sz, 40, ch), F32)],
        compiler_params=_cparams("parallel"),
        name="conv_sample",
    )(glu3d, state_all, w_pad, row(b_dw), row(ln_g), row(ln_b))
    return y, buf


def _kth_largest_rows(count_ge, shape, k_sel):
    def bit_step(b, r):
        trial = r + lax.shift_left(jnp.int32(1), jnp.int32(31) - b)
        return jnp.where(count_ge(trial) >= k_sel, trial, r)
    r = lax.fori_loop(0, 32, bit_step, jnp.full(shape, INT_MIN, jnp.int32))
    return jnp.maximum(r, INT_MIN + 1)


def _dsa_prompt_kernel(q_ref, iq_ref, iw_ref, ik_ref, k_ref, vt_ref, o_ref, key_scr, m_scr, l_scr, acc_scr, *,
                       tq, k_sel):
    i = pl.program_id(1)
    n_chunks = i + 1
    q0 = i * tq
    sub = LANES
    w_rows = iw_ref[...].T * (IDX_DIM ** -0.5 * IDX_HEADS ** -0.5)

    def score_chunk(c, carry):
        for s_i in range(tq // sub):
            r0 = pl.multiple_of(c * tq + s_i * sub, sub)
            ikc = ik_ref[pl.ds(r0, sub), :]
            acc = jnp.zeros((sub, tq), F32)
            for h in range(IDX_HEADS):
                s = _nt_dot(ikc, iq_ref[:, h * IDX_DIM:(h + 1) * IDX_DIM])
                acc = acc + jnp.maximum(s, 0.0) * w_rows[h:h + 1, :]
            kidx = r0 + lax.broadcasted_iota(jnp.int32, (sub, tq), 0)
            qidx = q0 + lax.broadcasted_iota(jnp.int32, (sub, tq), 1)
            key_scr[pl.ds(r0, sub), :] = jnp.where(kidx <= qidx, _order_key(acc), INT_MIN)
        return carry

    lax.fori_loop(0, n_chunks, score_chunk, 0)

    @pl.when((n_chunks % 2 == 1) & (n_chunks < pl.num_programs(1)))
    def _():
        key_scr[pl.ds(pl.multiple_of(n_chunks * tq, tq), tq), :] = jnp.full((tq, tq), INT_MIN, jnp.int32)

    def count_ge(trial):
        def cnt_pair(c, cnt):
            r0 = pl.multiple_of(c * 2 * tq, 2 * tq)
            m = jnp.where(key_scr[pl.ds(r0, 2 * tq), :] >= trial, 1, 0)
            return cnt + _fold_rows(jnp.add, m)
        cnt = lax.fori_loop(0, (n_chunks + 1) // 2, cnt_pair, jnp.zeros((SUBLANES, tq), jnp.int32))
        return cnt.sum(axis=0, keepdims=True)

    thr = _kth_largest_rows(count_ge, (1, tq), k_sel)

    scale = HEAD_DIM ** -0.5
    m_scr[...] = jnp.full(m_scr.shape, -jnp.inf, F32)
    l_scr[...] = jnp.zeros(l_scr.shape, F32)
    acc_scr[...] = jnp.zeros(acc_scr.shape, F32)
    head_cols = [slice(h * HEAD_DIM, (h + 1) * HEAD_DIM) for h in range(N_HEADS)]

    def attn_chunk(c, carry):
        r0 = pl.multiple_of(c * tq, tq)
        sel = key_scr[pl.ds(r0, tq), :] >= thr
        lgs = [_nt_dot(k_ref[pl.ds(r0, tq), hs], q_ref[:, hs]) for hs in head_cols]
        ps, alphas = [], []
        for h in range(N_HEADS):
            ss = slice(h * SUBLANES, (h + 1) * SUBLANES)
            lg = jnp.where(sel, lgs[h] * scale, -jnp.inf)
            mc = _fold_rows(jnp.maximum, lg).max(axis=0, keepdims=True)
            m_old = m_scr[ss, :]
            m_new = jnp.maximum(m_old, mc)
            m_safe = jnp.where(m_new == -jnp.inf, 0.0, m_new)
            alpha = jnp.exp(m_old - m_safe)
            p = jnp.exp(lg - m_safe[0:1, :])
            l_scr[ss, :] = alpha * l_scr[ss, :] + _fold_rows(jnp.add, p)
            m_scr[ss, :] = m_new
            ps.append(p.astype(BF16))
            alphas.append(alpha[0:1, :])
        for h, hs in enumerate(head_cols):
            pv = _nn_dot(vt_ref[hs, pl.ds(r0, tq)], ps[h])
            acc_scr[hs, :] = alphas[h] * acc_scr[hs, :] + pv
        return carry

    lax.fori_loop(0, n_chunks, attn_chunk, 0)

    for h in range(N_HEADS):
        hs = slice(h * HEAD_DIM, (h + 1) * HEAD_DIM)
        l = l_scr[h * SUBLANES:(h + 1) * SUBLANES, :].sum(axis=0, keepdims=True)
        o_ref[:, hs] = (acc_scr[hs, :] * (1.0 / l)).T.astype(o_ref.dtype)


def _dsa_prompt(q, iq, iw, ikb, kb, vt, batch, seq, tq=256):
    nq = seq // tq
    aw = N_HEADS * HEAD_DIM
    k_sel = min(TOPK_MAX, seq // 4)
    blk = lambda w: pl.BlockSpec((tq, w), lambda b, i: (b * nq + i, 0))
    return pl.pallas_call(
        functools.partial(_dsa_prompt_kernel, tq=tq, k_sel=k_sel),
        grid=(batch, nq),
        in_specs=[blk(aw), blk(IDX_HEADS * IDX_DIM), blk(LANES),
                  pl.BlockSpec((seq, IDX_DIM), lambda b, i: (b, 0)),
                  pl.BlockSpec((seq, aw), lambda b, i: (b, 0)),
                  pl.BlockSpec((None, aw, seq), lambda b, i: (b, 0, 0))],
        out_specs=blk(aw),
        out_shape=jax.ShapeDtypeStruct((batch * seq, aw), BF16),
        scratch_shapes=[pltpu.VMEM((seq, tq), jnp.int32),
                        pltpu.VMEM((N_HEADS * SUBLANES, tq), F32), pltpu.VMEM((N_HEADS * SUBLANES, tq), F32),
                        pltpu.VMEM((aw, tq), F32)],
        compiler_params=_cparams("parallel", "arbitrary"),
        name="dsa_prompt",
    )(q, iq, iw, ikb, kb, vt)


def _s_scores_kernel(pt_ref, iq_ref, iwc_ref, *rest, n_pages, t_new):
    pages, ikn_ref, o_ref = rest[:n_pages], rest[n_pages], rest[n_pages + 1]
    iq = iq_ref[...]
    wc = iwc_ref[...] * (IDX_DIM ** -0.5 * IDX_HEADS ** -0.5)
    o_ref[...] = jnp.full(o_ref.shape, -jnp.inf, F32)
    for j in range(n_pages + 1):
        ikp = pages[j][...].astype(BF16) if j < n_pages else ikn_ref[...]
        z = jnp.maximum(_nt_dot(iq, ikp), 0.0) * wc
        sc = z.reshape(t_new, IDX_HEADS, PAGE_SIZE).sum(axis=1)
        if j == n_pages:
            key_i = lax.broadcasted_iota(jnp.int32, sc.shape, 1)
            t_i = lax.broadcasted_iota(jnp.int32, sc.shape, 0)
            sc = jnp.where(key_i <= t_i, sc, -jnp.inf)
        o_ref[0:t_new, j * PAGE_SIZE:(j + 1) * PAGE_SIZE] = sc


def _s_scores(page_table, iq64, iwcol, cache_ik, layer, ikn_pad, t_new):
    db, n_pages = page_table.shape
    rows = t_new * IDX_HEADS
    width = (n_pages + 1) * PAGE_SIZE

    def page_spec(j):
        return pl.BlockSpec((None, None, PAGE_SIZE, IDX_DIM), lambda b, pt: (layer, pt[b, j], 0, 0))

    grid_spec = pltpu.PrefetchScalarGridSpec(
        num_scalar_prefetch=1,
        grid=(db,),
        in_specs=[pl.BlockSpec((rows, IDX_DIM), lambda b, pt: (b, 0)),
                  pl.BlockSpec((rows, 1), lambda b, pt: (b, 0))]
                 + [page_spec(j) for j in range(n_pages)]
                 + [pl.BlockSpec((None, PAGE_SIZE, IDX_DIM), lambda b, pt: (b, 0, 0))],
        out_specs=pl.BlockSpec((None, SUBLANES, width), lambda b, pt: (b, 0, 0)),
    )
    return pl.pallas_call(
        functools.partial(_s_scores_kernel, n_pages=n_pages, t_new=t_new),
        grid_spec=grid_spec,
        out_shape=jax.ShapeDtypeStruct((db, SUBLANES, width), F32),
        compiler_params=_cparams("arbitrary"),
        name="sample_scores",
    )(page_table, iq64, iwcol, *([cache_ik] * n_pages), ikn_pad)


def _s_select_kernel(s_ref, o_ref, key_scr, *, k_sel):
    s = s_ref[...]
    key_scr[...] = jnp.where(s == -jnp.inf, INT_MIN, _order_key(s))

    def count_ge(trial):
        return jnp.sum(jnp.where(key_scr[...] >= trial, 1.0, 0.0), axis=1, keepdims=True)

    thr = _kth_largest_rows(count_ge, (s.shape[0], 1), k_sel)
    o_ref[...] = jnp.where(key_scr[...] >= thr, 1.0, 0.0)


def _s_select(scores2d, k_sel, rows=256):
    m, width = scores2d.shape
    return pl.pallas_call(
        functools.partial(_s_select_kernel, k_sel=k_sel),
        grid=(m // rows,),
        in_specs=[pl.BlockSpec((rows, width), lambda i: (i, 0))],
        out_specs=pl.BlockSpec((rows, width), lambda i: (i, 0)),
        out_shape=jax.ShapeDtypeStruct((m, width), F32),
        scratch_shapes=[pltpu.VMEM((rows, width), jnp.int32)],
        compiler_params=_cparams("parallel"),
        name="sample_select",
    )(scores2d)


def _s_attn_kernel(pt_ref, q_ref, sel_ref, *rest, n_pages):
    kpages, vpages = rest[:n_pages], rest[n_pages:2 * n_pages]
    kn_ref, vn_ref, o_ref = rest[2 * n_pages:]
    rows = N_HEADS * SUBLANES
    kh = PAGE_SIZE * N_HEADS
    scale = HEAD_DIM ** -0.5

    qall = jnp.concatenate([q_ref[:, h * HEAD_DIM:(h + 1) * HEAD_DIM] for h in range(N_HEADS)], axis=0)
    lane_key = lax.broadcasted_iota(jnp.int32, (PAGE_SIZE, kh), 1) // N_HEADS
    expand = jnp.where(lane_key == lax.broadcasted_iota(jnp.int32, (PAGE_SIZE, kh), 0), 1.0, 0.0).astype(BF16)
    same_head = (lax.broadcasted_iota(jnp.int32, (rows, kh), 1) % N_HEADS
                 == lax.broadcasted_iota(jnp.int32, (rows, kh), 0) // SUBLANES)

    def flat(ref):
        return ref[...].reshape(-1, HEAD_DIM).astype(BF16)

    def masked_logits(kf, j, width):
        sel8 = sel_ref[:, j * PAGE_SIZE:(j + 1) * PAGE_SIZE].astype(BF16)
        picked = _nn_dot(jnp.concatenate([sel8] * N_HEADS, axis=0), expand[:, :width]) > 0.5
        return jnp.where(picked & same_head[:, :width], _nt_dot(qall, kf) * scale, -jnp.inf)

    zrows = jnp.zeros((LANES - N_HEADS * SUBLANES, HEAD_DIM), BF16)
    kn = jnp.concatenate([flat(kn_ref), zrows], axis=0)
    vn = jnp.concatenate([flat(vn_ref), zrows], axis=0)
    lgs = [masked_logits(flat(kpages[j]), j, kh) for j in range(n_pages)]
    lg_n = masked_logits(kn, n_pages, LANES)

    mx = lgs[0]
    for x in lgs[1:]:
        mx = jnp.maximum(mx, x)
    m = jnp.maximum(jnp.max(mx, axis=1, keepdims=True), jnp.max(lg_n, axis=1, keepdims=True))
    m = jnp.where(m == -jnp.inf, 0.0, m)
    p_n = jnp.exp(lg_n - m)
    l = jnp.sum(p_n, axis=1, keepdims=True)
    acc = _nn_dot(p_n.astype(BF16), vn)
    for j in range(n_pages):
        p = jnp.exp(lgs[j] - m)
        l = l + jnp.sum(p, axis=1, keepdims=True)
        acc = acc + _nn_dot(p.astype(BF16), flat(vpages[j]))
    out = acc / jnp.where(l == 0.0, 1.0, l)
    for h in range(N_HEADS):
        o_ref[:, h * HEAD_DIM:(h + 1) * HEAD_DIM] = out[h * SUBLANES:(h + 1) * SUBLANES, :].astype(o_ref.dtype)


def _s_attn(page_table, q8, sel, cache_k, cache_v, layer, kn8, vn8):
    db, n_pages = page_table.shape
    aw = N_HEADS * HEAD_DIM
    width = sel.shape[2]
    row_blk = pl.BlockSpec((None, SUBLANES, aw), lambda b, pt: (b, 0, 0))
    new_blk = pl.BlockSpec((None, SUBLANES, N_HEADS, HEAD_DIM), lambda b, pt: (b, 0, 0, 0))

    def page_blk(j):
        return pl.BlockSpec((None, None, PAGE_SIZE, N_HEADS, HEAD_DIM), lambda b, pt: (layer, pt[b, j], 0, 0, 0))

    pages = [page_blk(j) for j in range(n_pages)]
    grid_spec = pltpu.PrefetchScalarGridSpec(
        num_scalar_prefetch=1,
        grid=(db,),
        in_specs=[row_blk, pl.BlockSpec((None, SUBLANES, width), lambda b, pt: (b, 0, 0))]
                 + pages + pages + [new_blk, new_blk],
        out_specs=row_blk,
    )
    return pl.pallas_call(
        functools.partial(_s_attn_kernel, n_pages=n_pages),
        grid_spec=grid_spec,
        out_shape=jax.ShapeDtypeStruct((db, SUBLANES, aw), BF16),
        compiler_params=_cparams("arbitrary"),
        name="sample_attention",
    )(page_table, q8, sel, *([cache_k] * n_pages), *([cache_v] * n_pages), kn8, vn8)


def _merge_kernel(attn_ref, y_ref, gates_ref, x_ref, wa_ref, wc_ref, wo_ref, g_ref, o_ref):
    d = x_ref.shape[1]
    a = _nn_dot(attn_ref[...], wa_ref[...])
    c = _nn_dot(y_ref[...], wc_ref[...])
    mixed = jax.nn.sigmoid(gates_ref[:, :d]) * a + jax.nn.sigmoid(gates_ref[:, d:]) * c
    o = _nn_dot(mixed.astype(BF16), wo_ref[...])
    o_ref[...] = x_ref[...] + _rms(o, g_ref[...])


def _merge(attn, y, gates, x2d, wa, wc, wo, g, tm=256):
    m, d = x2d.shape
    aw, ch = attn.shape[1], y.shape[1]
    full = lambda r, c: pl.BlockSpec((r, c), lambda i: (0, 0))
    blk = lambda w: pl.BlockSpec((tm, w), lambda i: (i, 0))
    return pl.pallas_call(
        _merge_kernel,
        grid=(m // tm,),
        in_specs=[blk(aw), blk(ch), blk(2 * d), blk(d), full(aw, d), full(ch, d), full(d, d), full(1, d)],
        out_specs=blk(d),
        out_shape=jax.ShapeDtypeStruct((m, d), F32),
        compiler_params=_cparams("parallel"),
        name="merge",
    )(attn, y, gates, x2d, wa, wc, wo, g.reshape(1, d))


def _ffn_kernel(x_ref, gpre_ref, wu_ref, wd_ref, gpost_ref, o_ref, f_scr, acc_scr):
    f = pl.program_id(1)

    @pl.when(f == 0)
    def _():
        f_scr[...] = _rms(x_ref[...], gpre_ref[...]).astype(BF16)
        acc_scr[...] = jnp.zeros(acc_scr.shape, F32)

    hdn = jnp.square(jnp.maximum(_nn_dot(f_scr[...], wu_ref[...]), 0.0)).astype(BF16)
    acc_scr[...] += _nn_dot(hdn, wd_ref[...])

    @pl.when(f == pl.num_programs(1) - 1)
    def _():
        o_ref[...] = x_ref[...] + _rms(acc_scr[...], gpost_ref[...])


def _ffn(x2d, g_pre, w_up, w_down, g_post, tm=512, tf=512):
    m, d = x2d.shape
    d_ff = w_up.shape[1]
    vec = pl.BlockSpec((1, d), lambda i, f: (0, 0))
    return pl.pallas_call(
        _ffn_kernel,
        grid=(m // tm, d_ff // tf),
        in_specs=[pl.BlockSpec((tm, d), lambda i, f: (i, 0)), vec,
                  pl.BlockSpec((d, tf), lambda i, f: (0, f)),
                  pl.BlockSpec((tf, d), lambda i, f: (f, 0)), vec],
        out_specs=pl.BlockSpec((tm, d), lambda i, f: (i, 0)),
        out_shape=jax.ShapeDtypeStruct((m, d), F32),
        scratch_shapes=[pltpu.VMEM((tm, d), BF16), pltpu.VMEM((tm, d), F32)],
        compiler_params=_cparams("parallel", "arbitrary"),
        name="ffn",
    )(x2d, g_pre.reshape(1, d), w_up, w_down, g_post.reshape(1, d))


def kernel(x_prompt, x_sample, cache_k, cache_v, cache_idx_k, state_conv, page_table, w_in, w_attn_up, w_conv_dw,
           b_conv_dw, g_conv_ln, b_conv_ln, w_conv_out, w_o, w_ffn_up, w_ffn_down, g_pre_mix, g_post_mix,
           g_pre_ffn, g_post_ffn):
    depth = w_in.shape[0]
    batch, seq, d = x_prompt.shape
    db, t_new, _ = x_sample.shape
    n_pages = page_table.shape[1]
    aw = N_HEADS * HEAD_DIM
    ch = d // 2
    k_sel_s = min(TOPK_MAX, (n_pages * PAGE_SIZE + t_new) // 4)

    def pad_rows(a2d, rows):
        a3 = a2d.reshape(db, t_new, a2d.shape[1])
        return jnp.pad(a3, ((0, 0), (0, rows - t_new), (0, 0)))

    xp = x_prompt.reshape(batch * seq, d)
    xs = x_sample.reshape(db * t_new, d)
    outs = {n: [] for n in ("pk", "pv", "pik", "pconv", "sk", "sv", "sik", "sconv")}
    for l in range(depth):
        w_in_l = _pack_w_in(w_in[l:l + 1])[0]
        wa, wc, wo = w_attn_up[l].astype(BF16), w_conv_out[l].astype(BF16), w_o[l].astype(BF16)
        wu, wd = w_ffn_up[l].astype(BF16), w_ffn_down[l].astype(BF16)

        p = _in_projection(xp, g_pre_mix[l], w_in_l, tm=1024)
        vt = p["vb"].reshape(batch, seq, aw).transpose(0, 2, 1)
        attn = _dsa_prompt(p["q"], p["iq"], p["iw"], p["ikb"], p["kb"], vt, batch, seq)
        y, pbuf = _conv_prompt(p["glu"], w_conv_dw[l], b_conv_dw[l], g_conv_ln[l], b_conv_ln[l], batch, seq)
        xp = _merge(attn, y, p["gates"], xp, wa, wc, wo, g_post_mix[l])
        xp = _ffn(xp, g_pre_ffn[l], wu, wd, g_post_ffn[l])
        outs["pk"].append(p["k"].reshape(batch, seq, N_HEADS, HEAD_DIM))
        outs["pv"].append(p["v"].reshape(batch, seq, N_HEADS, HEAD_DIM))
        outs["pik"].append(p["ik"].reshape(batch, seq, IDX_DIM))
        outs["pconv"].append(pbuf)

        s = _in_projection(xs, g_pre_mix[l], w_in_l, tm=db * t_new)
        iq64 = s["iq"].reshape(db * t_new * IDX_HEADS, IDX_DIM)
        iwcol = s["iw"][:, :IDX_HEADS].reshape(db * t_new * IDX_HEADS, 1)
        ikn_pad = pad_rows(s["ikb"], PAGE_SIZE)
        scores = _s_scores(page_table, iq64, iwcol, cache_idx_k, l, ikn_pad, t_new)
        width = scores.shape[2]
        sel = _s_select(scores.reshape(db * SUBLANES, width), k_sel_s).reshape(db, SUBLANES, width)
        kn8 = pad_rows(s["k"], SUBLANES).reshape(db, SUBLANES, N_HEADS, HEAD_DIM)
        vn8 = pad_rows(s["v"], SUBLANES).reshape(db, SUBLANES, N_HEADS, HEAD_DIM)
        attn_s = _s_attn(page_table, pad_rows(s["q"], SUBLANES), sel, cache_k, cache_v, l, kn8, vn8)
        attn_s = attn_s[:, :t_new].reshape(db * t_new, aw)
        ys, sbuf = _conv_sample(s["glu"].reshape(db, t_new, 2 * ch), state_conv, l, w_conv_dw[l], b_conv_dw[l],
                                g_conv_ln[l], b_conv_ln[l])
        xs = _merge(attn_s, ys.reshape(db * t_new, ch), s["gates"], xs, wa, wc, wo, g_post_mix[l])
        xs = _ffn(xs, g_pre_ffn[l], wu, wd, g_post_ffn[l], tm=db * t_new)
        outs["sk"].append(s["k"].reshape(db, t_new, N_HEADS, HEAD_DIM))
        outs["sv"].append(s["v"].reshape(db, t_new, N_HEADS, HEAD_DIM))
        outs["sik"].append(s["ik"].reshape(db, t_new, IDX_DIM))
        outs["sconv"].append(sbuf)

    st = {n: jnp.stack(v) for n, v in outs.items()}
    return (xp.reshape(batch, seq, d), xs.reshape(db, t_new, d), st["pk"], st["pv"], st["pik"], st["pconv"],
            st["sk"], st["sv"], st["sik"], st["sconv"])
```

```python
import functools

import jax
import jax.numpy as jnp
import numpy as np
from jax import lax
from jax.experimental import pallas as pl
from jax.experimental.pallas import tpu as pltpu

N_HEADS = 8
HEAD_DIM = 128
IDX_HEADS = 16
IDX_DIM = 128
TOPK_MAX = 256
CONV_WIDTH = 31
CONV_BUF = CONV_WIDTH - 1
PAGE_SIZE = 128
NORM_EPS = 1e-6
LN_EPS = 1e-5

LANES = 128
SUBLANES = 8
VMEM_LIMIT = 56 * 1024 * 1024

INT_MIN = np.int32(-2 ** 31)
BF16 = jnp.bfloat16
F32 = jnp.float32


def _cparams(*sem):
    return pltpu.CompilerParams(dimension_semantics=sem, vmem_limit_bytes=VMEM_LIMIT)


def _rms(x, g):
    ms = jnp.mean(x * x, axis=-1, keepdims=True)
    return x * lax.rsqrt(ms + NORM_EPS) * g


def _nt_dot(a, b):
    return lax.dot_general(a, b, (((1,), (1,)), ((), ())), preferred_element_type=F32)


def _nn_dot(a, b):
    return jnp.dot(a, b, preferred_element_type=F32)


def _fold_rows(op, x, ways=4):
    parts = [x[r:r + SUBLANES] for r in range(0, x.shape[0], SUBLANES)]
    accs = parts[:ways]
    for i, part in enumerate(parts[ways:]):
        accs[i % ways] = op(accs[i % ways], part)
    while len(accs) > 1:
        accs = [op(a, b) for a, b in zip(accs[0::2], accs[1::2])] + (accs[-1:] if len(accs) % 2 else [])
    return accs[0]


def _order_key(x):
    bits = pltpu.bitcast(x, jnp.int32)
    return bits ^ ((bits >> 31) & jnp.int32(0x7FFFFFFF))


_TN = 512
_SEGS = (("q", 2), ("k", 2), ("v", 2), ("iq", 4), ("glu", 4), ("gates", 8), ("small", 1))
_SEG_START = {}
_acc = 0
for _n, _c in _SEGS:
    _SEG_START[_n] = _acc
    _acc += _c
_N_COL_TILES = _acc
_SEG_COUNT = dict(_SEGS)


def _split_w_in(w_in):
    aw = N_HEADS * HEAD_DIM
    ik0 = 3 * aw + IDX_HEADS * IDX_DIM
    iw0 = ik0 + IDX_DIM
    head = w_in[:, :, :iw0].astype(BF16)
    tail = w_in[:, :, iw0 + IDX_HEADS:].astype(BF16)
    iw = jnp.pad(w_in[:, :, iw0:iw0 + IDX_HEADS], ((0, 0), (0, 0), (0, LANES - IDX_HEADS))).astype(BF16)
    return head, tail, iw


def _inproj_kernel(x_ref, g_ref, wh_ref, wik_ref, wt_ref, wiw_ref, *rest, n_alias):
    (q_ref, k_ref, kb_ref, v_ref, vb_ref, iq_ref, glu_ref, gates_ref, ik_ref, ikb_ref, iw_ref,
     h_scr) = rest[n_alias:]
    j = pl.program_id(1)

    @pl.when(j == 0)
    def _():
        h_scr[...] = _rms(x_ref[...], g_ref[...]).astype(BF16)

    def seg(name):
        s = _SEG_START[name]
        return (j >= s) & (j < s + _SEG_COUNT[name])

    def proj(w_ref):
        return _nn_dot(h_scr[...], w_ref[...])

    @pl.when(seg("q"))
    def _():
        q_ref[...] = proj(wh_ref).astype(BF16)

    @pl.when(seg("k"))
    def _():
        r = proj(wh_ref)
        k_ref[...] = r
        kb_ref[...] = r.astype(BF16)

    @pl.when(seg("v"))
    def _():
        r = proj(wh_ref)
        v_ref[...] = r
        vb_ref[...] = r.astype(BF16)

    @pl.when(seg("iq"))
    def _():
        iq_ref[...] = proj(wh_ref).astype(BF16)

    @pl.when(seg("glu"))
    def _():
        glu_ref[...] = proj(wt_ref).astype(BF16)

    @pl.when(seg("gates"))
    def _():
        gates_ref[...] = proj(wt_ref).astype(BF16)

    @pl.when(seg("small"))
    def _():
        r = proj(wik_ref)
        ik_ref[...] = r
        ikb_ref[...] = r.astype(BF16)
        iw_ref[...] = proj(wiw_ref)


def _in_projection(x2d, g, w_split, layer, depth, stacked_prev, tm):
    m, d = x2d.shape
    aw = N_HEADS * HEAD_DIM
    w_head, w_tail, w_iw = w_split
    head_tiles = _SEG_START["glu"]
    tail_tiles = _SEG_COUNT["glu"] + _SEG_COUNT["gates"]

    def col_idx(name):
        s, c = _SEG_START[name], _SEG_COUNT[name]
        return lambda j: jnp.clip(j - s, 0, c - 1)

    def out(name, width_total, dtype, block_w=_TN):
        ci = col_idx(name)
        return (jax.ShapeDtypeStruct((m, width_total), dtype),
                pl.BlockSpec((tm, block_w), lambda i, j: (i, ci(j))))

    def out_stacked(name, width_total, block_w=_TN):
        ci = col_idx(name)
        return (jax.ShapeDtypeStruct((depth, m, width_total), F32),
                pl.BlockSpec((None, tm, block_w), lambda i, j: (layer, i, ci(j))))

    outs = [
        out("q", aw, BF16), out_stacked("k", aw), out("k", aw, BF16), out_stacked("v", aw), out("v", aw, BF16),
        out("iq", IDX_HEADS * IDX_DIM, BF16), out("glu", d, BF16), out("gates", 2 * d, BF16),
        out_stacked("small", IDX_DIM, IDX_DIM), out("small", IDX_DIM, BF16, IDX_DIM),
        out("small", LANES, F32, LANES),
    ]
    stacked_out_idx = (1, 3, 8)
    n_alias = 0 if stacked_prev is None else len(stacked_prev)
    n_in = 6
    aliases = {n_in + a: stacked_out_idx[a] for a in range(n_alias)}
    res = pl.pallas_call(
        functools.partial(_inproj_kernel, n_alias=n_alias),
        grid=(m // tm, _N_COL_TILES),
        in_specs=[pl.BlockSpec((tm, d), lambda i, j: (i, 0), pipeline_mode=pl.Buffered(1)),
                  pl.BlockSpec((1, d), lambda i, j: (0, 0)),
                  pl.BlockSpec((None, d, _TN), lambda i, j: (layer, 0, jnp.clip(j, 0, head_tiles - 1))),
                  pl.BlockSpec((None, d, IDX_DIM), lambda i, j: (layer, 0, head_tiles * _TN // IDX_DIM)),
                  pl.BlockSpec((None, d, _TN), lambda i, j: (layer, 0, jnp.clip(j - head_tiles, 0, tail_tiles - 1))),
                  pl.BlockSpec((None, d, LANES), lambda i, j: (layer, 0, 0))]
                 + [pl.BlockSpec(memory_space=pl.ANY)] * n_alias,
        out_specs=[o[1] for o in outs],
        out_shape=[o[0] for o in outs],
        input_output_aliases=aliases,
        scratch_shapes=[pltpu.VMEM((tm, d), BF16)],
        compiler_params=_cparams("parallel", "arbitrary"),
        name="in_projection",
    )(x2d, g.reshape(1, d), w_head, w_head, w_tail, w_iw, *(stacked_prev or ()))
    names = ("q", "k", "kb", "v", "vb", "iq", "glu", "gates", "ik", "ikb", "iw")
    return dict(zip(names, res))


_TAIL = 32


def _ln_swish(y, g, b):
    mu = jnp.mean(y, axis=-1, keepdims=True)
    yc = y - mu
    var = jnp.mean(yc * yc, axis=-1, keepdims=True)
    z = yc * lax.rsqrt(var + LN_EPS) * g + b
    return z * jax.nn.sigmoid(z)


def _conv_prompt_kernel(glu_ref, w_ref, bdw_ref, lng_ref, lnb_ref, y_ref, buf_ref, ext_scr, shift_scr, acc_scr, *,
                        tt, ch):
    i = pl.program_id(1)

    @pl.when(i == 0)
    def _():
        ext_scr[0:_TAIL, :] = jnp.zeros((_TAIL, ch), F32)

    a = glu_ref[:, :ch].astype(F32)
    g = glu_ref[:, ch:].astype(F32)
    ext_scr[_TAIL:, :] = a * jax.nn.sigmoid(g)

    off0 = _TAIL - CONV_BUF
    rows = 64
    span = tt + _TAIL - SUBLANES
    for r in range(1, SUBLANES):
        shift_scr[r, 0:span, :] = ext_scr[r:r + span, :]

    def shifted(r, start):
        return ext_scr[start:start + rows, cs] if r == 0 else shift_scr[r, start:start + rows, cs]

    for c in range(ch // LANES):
        cs = slice(c * LANES, (c + 1) * LANES)
        wc = w_ref[:, cs]
        for rc in range(tt // rows):
            acc = jnp.zeros((rows, LANES), F32)
            for j in range(CONV_WIDTH):
                a, r = divmod(off0 + j, SUBLANES)
                acc = acc + shifted(r, rc * rows + SUBLANES * a) * wc[j:j + 1, :]
            acc_scr[rc * rows:(rc + 1) * rows, cs] = acc

    y = acc_scr[...] + bdw_ref[...]
    y_ref[...] = _ln_swish(y, lng_ref[...], lnb_ref[...]).astype(y_ref.dtype)

    @pl.when(i == pl.num_programs(1) - 1)
    def _():
        buf_ref[...] = ext_scr[tt + _TAIL - CONV_BUF: tt + _TAIL, :]

    ext_scr[0:_TAIL, :] = ext_scr[tt:tt + _TAIL, :]


def _conv_prompt(glu2d, w_dw, b_dw, ln_g, ln_b, batch, seq, tt=256):
    ch = w_dw.shape[1]
    nt = seq // tt
    row = lambda a: a.reshape(1, ch)
    vec = pl.BlockSpec((1, ch), lambda b, i: (0, 0))
    y, buf = pl.pallas_call(
        functools.partial(_conv_prompt_kernel, tt=tt, ch=ch),
        grid=(batch, nt),
        in_specs=[pl.BlockSpec((tt, 2 * ch), lambda b, i: (b * nt + i, 0)),
                  pl.BlockSpec((CONV_WIDTH, ch), lambda b, i: (0, 0)), vec, vec, vec],
        out_specs=[pl.BlockSpec((tt, ch), lambda b, i: (b * nt + i, 0)),
                   pl.BlockSpec((None, CONV_BUF, ch), lambda b, i: (b, 0, 0))],
        out_shape=[jax.ShapeDtypeStruct((batch * seq, ch), BF16),
                   jax.ShapeDtypeStruct((batch, CONV_BUF, ch), F32)],
        scratch_shapes=[pltpu.VMEM((tt + _TAIL, ch), F32), pltpu.VMEM((SUBLANES, tt + _TAIL, ch), F32),
                        pltpu.VMEM((tt, ch), F32)],
        compiler_params=_cparams("parallel", "arbitrary"),
        name="conv_prompt",
    )(glu2d, w_dw, row(b_dw), row(ln_g), row(ln_b))
    return y, buf


def _conv_sample_kernel(glu_ref, st_ref, w_ref, bdw_ref, lng_ref, lnb_ref, y_ref, buf_ref, full_scr, *, t_new, ch):
    g_sz = st_ref.shape[0]
    a = glu_ref[:, :, :ch].astype(F32)
    g = glu_ref[:, :, ch:].astype(F32)
    full_scr[:, 0:CONV_BUF, :] = st_ref[...]
    full_scr[:, CONV_BUF:CONV_BUF + t_new, :] = a * jax.nn.sigmoid(g)
    pad0 = CONV_BUF + t_new
    full_scr[:, pad0:, :] = jnp.zeros((g_sz, full_scr.shape[1] - pad0, ch), F32)
    w = w_ref[...]
    for t in range(t_new):
        y = jnp.sum(full_scr[:, t:t + CONV_WIDTH + 1, :] * w[None], axis=1) + bdw_ref[...]
        y_ref[:, t, :] = _ln_swish(y, lng_ref[...], lnb_ref[...]).astype(y_ref.dtype)
    buf_ref[...] = full_scr[:, t_new:t_new + CONV_BUF, :]


def _conv_sample(glu3d, state_all, layer, w_dw, b_dw, ln_g, ln_b, g_sz=16):
    db, t_new, _ = glu3d.shape
    ch = w_dw.shape[1]
    row = lambda a: a.reshape(1, ch)
    vec = pl.BlockSpec((1, ch), lambda b: (0, 0))
    w_pad = jnp.concatenate([w_dw, jnp.zeros((1, ch), w_dw.dtype)], axis=0)
    y, buf = pl.pallas_call(
        functools.partial(_conv_sample_kernel, t_new=t_new, ch=ch),
        grid=(db // g_sz,),
        in_specs=[pl.BlockSpec((g_sz, t_new, 2 * ch), lambda b: (b, 0, 0)),
                  pl.BlockSpec((None, g_sz, CONV_BUF, ch), lambda b: (layer, b, 0, 0)),
                  pl.BlockSpec((CONV_WIDTH + 1, ch), lambda b: (0, 0)), vec, vec, vec],
        out_specs=[pl.BlockSpec((g_sz, t_new, ch), lambda b: (b, 0, 0)),
                   pl.BlockSpec((g_sz, CONV_BUF, ch), lambda b: (b, 0, 0))],
        out_shape=[jax.ShapeDtypeStruct((db, t_new, ch), BF16),
                   jax.ShapeDtypeStruct((db, CONV_BUF, ch), F32)],
        scratch_shapes=[pltpu.VMEM((g_sz, 40, ch), F32)],
        compiler_params=_cparams("parallel"),
        name="conv_sample",
    )(glu3d, state_all, w_pad, row(b_dw), row(ln_g), row(ln_b))
    return y, buf


def _topk_threshold(count_ge, r0, n_bits, k_sel):
    def cond(state):
        bit, _, cnt = state
        return (bit >= 0) & (jnp.max(cnt.astype(F32)) > k_sel)

    def body(state):
        bit, r, cnt = state
        trial = r + lax.shift_left(jnp.int32(1), bit)
        c = count_ge(trial)
        take = c >= k_sel
        return bit - 1, jnp.where(take, trial, r), jnp.where(take, c, cnt)

    cnt0 = count_ge(jnp.maximum(r0, INT_MIN + 1))
    _, r, _ = lax.while_loop(cond, body, (n_bits - 1, r0, cnt0))
    return jnp.maximum(r, INT_MIN + 1)


def _dsa_prompt_kernel(q_ref, iq_ref, iw_ref, ik_ref, k_ref, vt_ref, o_ref, key_scr, gmax_scr, m_scr, l_scr, acc_scr,
                       *, tq, k_sel):
    i = pl.program_id(1)
    n_chunks = i + 1
    q0 = i * tq
    sub = LANES
    w_rows = iw_ref[...].T * (IDX_DIM ** -0.5 * IDX_HEADS ** -0.5)

    def score_chunk(c, carry):
        for s_i in range(tq // sub):
            r0 = pl.multiple_of(c * tq + s_i * sub, sub)
            ikc = ik_ref[pl.ds(r0, sub), :]
            acc = jnp.zeros((sub, tq), F32)
            for h in range(IDX_HEADS):
                s = _nt_dot(ikc, iq_ref[:, h * IDX_DIM:(h + 1) * IDX_DIM])
                acc = acc + jnp.maximum(s, 0.0) * w_rows[h:h + 1, :]
            kidx = r0 + lax.broadcasted_iota(jnp.int32, (sub, tq), 0)
            qidx = q0 + lax.broadcasted_iota(jnp.int32, (sub, tq), 1)
            key = jnp.where(kidx <= qidx, _order_key(acc), INT_MIN)
            key_scr[pl.ds(r0, sub), :] = key
            gs = slice(s_i * sub, (s_i + 1) * sub)
            gmax_scr[gs, :] = jnp.maximum(gmax_scr[gs, :], key)
        return carry

    gmax_scr[...] = jnp.full(gmax_scr.shape, INT_MIN, jnp.int32)
    lax.fori_loop(0, n_chunks, score_chunk, 0)
    gmax = gmax_scr[...]
    lo = _fold_rows(jnp.minimum, gmax).min(axis=0, keepdims=True)
    hi = _fold_rows(jnp.maximum, gmax).max(axis=0, keepdims=True)
    if tq >= k_sel:
        n_bits = 32 - jnp.min(lax.clz(lo ^ hi).astype(F32)).astype(jnp.int32)
        sh = jnp.minimum(n_bits, 31)
        r_start = jnp.where(n_bits >= 32, INT_MIN, (lo >> sh) << sh)
    else:
        n_bits = jnp.int32(32)
        r_start = jnp.full((1, tq), INT_MIN, jnp.int32)

    @pl.when((n_chunks % 2 == 1) & (n_chunks < pl.num_programs(1)))
    def _():
        key_scr[pl.ds(pl.multiple_of(n_chunks * tq, tq), tq), :] = jnp.full((tq, tq), INT_MIN, jnp.int32)

    def count_ge(trial):
        def cnt_pair(c, cnt):
            r0 = pl.multiple_of(c * 2 * tq, 2 * tq)
            m = jnp.where(key_scr[pl.ds(r0, 2 * tq), :] >= trial, 1, 0)
            return cnt + _fold_rows(jnp.add, m)
        cnt = lax.fori_loop(0, (n_chunks + 1) // 2, cnt_pair, jnp.zeros((SUBLANES, tq), jnp.int32))
        return cnt.sum(axis=0, keepdims=True)

    thr = _topk_threshold(count_ge, r_start, n_bits, k_sel)

    scale = HEAD_DIM ** -0.5 * float(np.log2(np.e))
    m_scr[...] = jnp.full(m_scr.shape, -jnp.inf, F32)
    l_scr[...] = jnp.zeros(l_scr.shape, F32)
    acc_scr[...] = jnp.zeros(acc_scr.shape, F32)
    head_cols = [slice(h * HEAD_DIM, (h + 1) * HEAD_DIM) for h in range(N_HEADS)]

    def attn_chunk(c, carry):
        r0 = pl.multiple_of(c * tq, tq)
        sel = key_scr[pl.ds(r0, tq), :] >= thr
        lgs = [_nt_dot(k_ref[pl.ds(r0, tq), hs], q_ref[:, hs]) for hs in head_cols]
        ps, alphas = [], []
        for h in range(N_HEADS):
            ss = slice(h * SUBLANES, (h + 1) * SUBLANES)
            lg = jnp.where(sel, lgs[h] * scale, -jnp.inf)
            mc = _fold_rows(jnp.maximum, lg).max(axis=0, keepdims=True)
            m_old = m_scr[ss, :]
            m_new = jnp.maximum(m_old, mc)
            m_safe = jnp.where(m_new == -jnp.inf, 0.0, m_new)
            alpha = jnp.exp2(m_old - m_safe)
            p = jnp.exp2(lg - m_safe[0:1, :])
            l_scr[ss, :] = alpha * l_scr[ss, :] + _fold_rows(jnp.add, p)
            m_scr[ss, :] = m_new
            ps.append(p.astype(BF16))
            alphas.append(alpha[0:1, :])
        for h, hs in enumerate(head_cols):
            pv = _nn_dot(vt_ref[hs, pl.ds(r0, tq)], ps[h])
            acc_scr[hs, :] = alphas[h] * acc_scr[hs, :] + pv
        return carry

    lax.fori_loop(0, n_chunks, attn_chunk, 0)

    for h in range(N_HEADS):
        hs = slice(h * HEAD_DIM, (h + 1) * HEAD_DIM)
        l = l_scr[h * SUBLANES:(h + 1) * SUBLANES, :].sum(axis=0, keepdims=True)
        o_ref[:, hs] = (acc_scr[hs, :] * (1.0 / l)).T.astype(o_ref.dtype)


def _dsa_prompt(q, iq, iw, ikb, kb, vt, batch, seq, tq=256):
    nq = seq // tq
    aw = N_HEADS * HEAD_DIM
    k_sel = min(TOPK_MAX, seq // 4)
    blk = lambda w: pl.BlockSpec((tq, w), lambda b, i: (b * nq + i, 0))
    return pl.pallas_call(
        functools.partial(_dsa_prompt_kernel, tq=tq, k_sel=k_sel),
        grid=(batch, nq),
        in_specs=[blk(aw), blk(IDX_HEADS * IDX_DIM), blk(LANES),
                  pl.BlockSpec((seq, IDX_DIM), lambda b, i: (b, 0)),
                  pl.BlockSpec((seq, aw), lambda b, i: (b, 0)),
                  pl.BlockSpec((None, aw, seq), lambda b, i: (b, 0, 0))],
        out_specs=blk(aw),
        out_shape=jax.ShapeDtypeStruct((batch * seq, aw), BF16),
        scratch_shapes=[pltpu.VMEM((seq, tq), jnp.int32), pltpu.VMEM((tq, tq), jnp.int32),
                        pltpu.VMEM((N_HEADS * SUBLANES, tq), F32), pltpu.VMEM((N_HEADS * SUBLANES, tq), F32),
                        pltpu.VMEM((aw, tq), F32)],
        compiler_params=_cparams("parallel", "arbitrary"),
        name="dsa_prompt",
    )(q, iq, iw, ikb, kb, vt)


def _s_scores_kernel(pt_ref, iq_ref, iwc_ref, *rest, n_pages, t_new):
    pages, ikn_ref, o_ref = rest[:n_pages], rest[n_pages], rest[n_pages + 1]
    iq = iq_ref[...]
    wc = iwc_ref[...] * (IDX_DIM ** -0.5 * IDX_HEADS ** -0.5)
    o_ref[...] = jnp.full(o_ref.shape, -jnp.inf, F32)
    for j in range(n_pages + 1):
        ikp = pages[j][...].astype(BF16) if j < n_pages else ikn_ref[...]
        z = jnp.maximum(_nt_dot(iq, ikp), 0.0) * wc
        sc = z.reshape(t_new, IDX_HEADS, PAGE_SIZE).sum(axis=1)
        if j == n_pages:
            key_i = lax.broadcasted_iota(jnp.int32, sc.shape, 1)
            t_i = lax.broadcasted_iota(jnp.int32, sc.shape, 0)
            sc = jnp.where(key_i <= t_i, sc, -jnp.inf)
        o_ref[0:t_new, j * PAGE_SIZE:(j + 1) * PAGE_SIZE] = sc


def _s_scores(page_table, iq64, iwcol, cache_ik, layer, ikn_pad, t_new):
    db, n_pages = page_table.shape
    rows = t_new * IDX_HEADS
    width = (n_pages + 1) * PAGE_SIZE

    def page_spec(j):
        return pl.BlockSpec((None, None, PAGE_SIZE, IDX_DIM), lambda b, pt: (layer, pt[b, j], 0, 0))

    grid_spec = pltpu.PrefetchScalarGridSpec(
        num_scalar_prefetch=1,
        grid=(db,),
        in_specs=[pl.BlockSpec((rows, IDX_DIM), lambda b, pt: (b, 0)),
                  pl.BlockSpec((rows, 1), lambda b, pt: (b, 0))]
                 + [page_spec(j) for j in range(n_pages)]
                 + [pl.BlockSpec((None, PAGE_SIZE, IDX_DIM), lambda b, pt: (b, 0, 0))],
        out_specs=pl.BlockSpec((None, SUBLANES, width), lambda b, pt: (b, 0, 0)),
    )
    return pl.pallas_call(
        functools.partial(_s_scores_kernel, n_pages=n_pages, t_new=t_new),
        grid_spec=grid_spec,
        out_shape=jax.ShapeDtypeStruct((db, SUBLANES, width), F32),
        compiler_params=_cparams("arbitrary"),
        name="sample_scores",
    )(page_table, iq64, iwcol, *([cache_ik] * n_pages), ikn_pad)


def _s_select_kernel(s_ref, o_ref, key_scr, *, k_sel):
    s = s_ref[...]
    key_scr[...] = jnp.where(s == -jnp.inf, INT_MIN, _order_key(s))

    def count_ge(trial):
        return jnp.sum(jnp.where(key_scr[...] >= trial, 1.0, 0.0), axis=1, keepdims=True).astype(jnp.int32)

    thr = _topk_threshold(count_ge, jnp.full((s.shape[0], 1), INT_MIN, jnp.int32), jnp.int32(32), k_sel)
    o_ref[...] = jnp.where(key_scr[...] >= thr, 1.0, 0.0)


def _s_select(scores2d, k_sel, rows=256):
    m, width = scores2d.shape
    return pl.pallas_call(
        functools.partial(_s_select_kernel, k_sel=k_sel),
        grid=(m // rows,),
        in_specs=[pl.BlockSpec((rows, width), lambda i: (i, 0))],
        out_specs=pl.BlockSpec((rows, width), lambda i: (i, 0)),
        out_shape=jax.ShapeDtypeStruct((m, width), F32),
        scratch_shapes=[pltpu.VMEM((rows, width), jnp.int32)],
        compiler_params=_cparams("parallel"),
        name="sample_select",
    )(scores2d)


def _s_attn_kernel(pt_ref, q_ref, sel_ref, *rest, n_pages):
    kpages, vpages = rest[:n_pages], rest[n_pages:2 * n_pages]
    kn_ref, vn_ref, o_ref = rest[2 * n_pages:]
    rows = N_HEADS * SUBLANES
    kh = PAGE_SIZE * N_HEADS
    scale = HEAD_DIM ** -0.5 * float(np.log2(np.e))

    qall = jnp.concatenate([q_ref[:, h * HEAD_DIM:(h + 1) * HEAD_DIM] for h in range(N_HEADS)], axis=0)
    lane_key = lax.broadcasted_iota(jnp.int32, (PAGE_SIZE, kh), 1) // N_HEADS
    expand = jnp.where(lane_key == lax.broadcasted_iota(jnp.int32, (PAGE_SIZE, kh), 0), 1.0, 0.0).astype(BF16)
    same_head = (lax.broadcasted_iota(jnp.int32, (rows, kh), 1) % N_HEADS
                 == lax.broadcasted_iota(jnp.int32, (rows, kh), 0) // SUBLANES)

    def flat(ref):
        return ref[...].reshape(-1, HEAD_DIM).astype(BF16)

    def masked_logits(kf, j, width):
        sel8 = sel_ref[:, j * PAGE_SIZE:(j + 1) * PAGE_SIZE].astype(BF16)
        picked = _nn_dot(jnp.concatenate([sel8] * N_HEADS, axis=0), expand[:, :width]) > 0.5
        return jnp.where(picked & same_head[:, :width], _nt_dot(qall, kf) * scale, -jnp.inf)

    zrows = jnp.zeros((LANES - N_HEADS * SUBLANES, HEAD_DIM), BF16)
    kn = jnp.concatenate([flat(kn_ref), zrows], axis=0)
    vn = jnp.concatenate([flat(vn_ref), zrows], axis=0)
    lgs = [masked_logits(flat(kpages[j]), j, kh) for j in range(n_pages)]
    lg_n = masked_logits(kn, n_pages, LANES)

    mx = lgs[0]
    for x in lgs[1:]:
        mx = jnp.maximum(mx, x)
    m = jnp.maximum(jnp.max(mx, axis=1, keepdims=True), jnp.max(lg_n, axis=1, keepdims=True))
    m = jnp.where(m == -jnp.inf, 0.0, m)
    p_n = jnp.exp2(lg_n - m)
    l = jnp.sum(p_n, axis=1, keepdims=True)
    acc = _nn_dot(p_n.astype(BF16), vn)
    for j in range(n_pages):
        p = jnp.exp2(lgs[j] - m)
        l = l + jnp.sum(p, axis=1, keepdims=True)
        acc = acc + _nn_dot(p.astype(BF16), flat(vpages[j]))
    out = acc / jnp.where(l == 0.0, 1.0, l)
    for h in range(N_HEADS):
        o_ref[:, h * HEAD_DIM:(h + 1) * HEAD_DIM] = out[h * SUBLANES:(h + 1) * SUBLANES, :].astype(o_ref.dtype)


def _s_attn(page_table, q8, sel, cache_k, cache_v, layer, kn8, vn8):
    db, n_pages = page_table.shape
    aw = N_HEADS * HEAD_DIM
    width = sel.shape[2]
    row_blk = pl.BlockSpec((None, SUBLANES, aw), lambda b, pt: (b, 0, 0))
    new_blk = pl.BlockSpec((None, SUBLANES, N_HEADS, HEAD_DIM), lambda b, pt: (b, 0, 0, 0))

    def page_blk(j):
        return pl.BlockSpec((None, None, PAGE_SIZE, N_HEADS, HEAD_DIM), lambda b, pt: (layer, pt[b, j], 0, 0, 0))

    pages = [page_blk(j) for j in range(n_pages)]
    grid_spec = pltpu.PrefetchScalarGridSpec(
        num_scalar_prefetch=1,
        grid=(db,),
        in_specs=[row_blk, pl.BlockSpec((None, SUBLANES, width), lambda b, pt: (b, 0, 0))]
                 + pages + pages + [new_blk, new_blk],
        out_specs=row_blk,
    )
    return pl.pallas_call(
        functools.partial(_s_attn_kernel, n_pages=n_pages),
        grid_spec=grid_spec,
        out_shape=jax.ShapeDtypeStruct((db, SUBLANES, aw), BF16),
        compiler_params=_cparams("arbitrary"),
        name="sample_attention",
    )(page_table, q8, sel, *([cache_k] * n_pages), *([cache_v] * n_pages), kn8, vn8)


def _merge_kernel(attn_ref, y_ref, gates_ref, x_ref, wa_ref, wc_ref, wo_ref, g_ref, o_ref):
    d = x_ref.shape[1]
    a = _nn_dot(attn_ref[...], wa_ref[...])
    c = _nn_dot(y_ref[...], wc_ref[...])
    g_a = jax.nn.sigmoid(gates_ref[:, :d].astype(F32))
    g_c = jax.nn.sigmoid(gates_ref[:, d:].astype(F32))
    mixed = g_a * a + g_c * c
    o = _nn_dot(mixed.astype(BF16), wo_ref[...])
    o_ref[...] = x_ref[...] + _rms(o, g_ref[...])


def _merge(attn, y, gates, x2d, wa, wc, wo, layer, g, tm=256):
    m, d = x2d.shape
    aw, ch = attn.shape[1], y.shape[1]
    w_full = lambda r, c: pl.BlockSpec((None, r, c), lambda i: (layer, 0, 0))
    blk = lambda w: pl.BlockSpec((tm, w), lambda i: (i, 0))
    return pl.pallas_call(
        _merge_kernel,
        grid=(m // tm,),
        in_specs=[blk(aw), blk(ch), blk(2 * d), blk(d), w_full(aw, d), w_full(ch, d), w_full(d, d),
                  pl.BlockSpec((1, d), lambda i: (0, 0))],
        out_specs=blk(d),
        out_shape=jax.ShapeDtypeStruct((m, d), F32),
        compiler_params=_cparams("parallel"),
        name="merge",
    )(attn, y, gates, x2d, wa, wc, wo, g.reshape(1, d))


def _ffn_kernel(x_ref, gpre_ref, wu_ref, wd_ref, gpost_ref, o_ref, f_scr, acc_scr):
    f = pl.program_id(1)

    @pl.when(f == 0)
    def _():
        f_scr[...] = _rms(x_ref[...], gpre_ref[...]).astype(BF16)
        acc_scr[...] = jnp.zeros(acc_scr.shape, F32)

    hdn = jnp.square(jnp.maximum(_nn_dot(f_scr[...], wu_ref[...]), 0.0)).astype(BF16)
    acc_scr[...] += _nn_dot(hdn, wd_ref[...])

    @pl.when(f == pl.num_programs(1) - 1)
    def _():
        o_ref[...] = x_ref[...] + _rms(acc_scr[...], gpost_ref[...])


def _ffn(x2d, g_pre, w_up, w_down, layer, g_post, tm=512, tf=512):
    m, d = x2d.shape
    d_ff = w_up.shape[2]
    vec = pl.BlockSpec((1, d), lambda i, f: (0, 0))
    return pl.pallas_call(
        _ffn_kernel,
        grid=(m // tm, d_ff // tf),
        in_specs=[pl.BlockSpec((tm, d), lambda i, f: (i, 0)), vec,
                  pl.BlockSpec((None, d, tf), lambda i, f: (layer, 0, f)),
                  pl.BlockSpec((None, tf, d), lambda i, f: (layer, f, 0)), vec],
        out_specs=pl.BlockSpec((tm, d), lambda i, f: (i, 0)),
        out_shape=jax.ShapeDtypeStruct((m, d), F32),
        scratch_shapes=[pltpu.VMEM((tm, d), BF16), pltpu.VMEM((tm, d), F32)],
        compiler_params=_cparams("parallel", "arbitrary"),
        name="ffn",
    )(x2d, g_pre.reshape(1, d), w_up, w_down, g_post.reshape(1, d))


def kernel(x_prompt, x_sample, cache_k, cache_v, cache_idx_k, state_conv, page_table, w_in, w_attn_up, w_conv_dw,
           b_conv_dw, g_conv_ln, b_conv_ln, w_conv_out, w_o, w_ffn_up, w_ffn_down, g_pre_mix, g_post_mix,
           g_pre_ffn, g_post_ffn):
    depth = w_in.shape[0]
    batch, seq, d = x_prompt.shape
    db, t_new, _ = x_sample.shape
    n_pages = page_table.shape[1]
    aw = N_HEADS * HEAD_DIM
    ch = d // 2
    k_sel_s = min(TOPK_MAX, (n_pages * PAGE_SIZE + t_new) // 4)

    def pad_rows(a2d, rows):
        a3 = a2d.reshape(db, t_new, a2d.shape[1])
        return jnp.pad(a3, ((0, 0), (0, rows - t_new), (0, 0)))

    w_split = _split_w_in(w_in)
    wa, wc, wo = w_attn_up.astype(BF16), w_conv_out.astype(BF16), w_o.astype(BF16)
    wu, wd = w_ffn_up.astype(BF16), w_ffn_down.astype(BF16)

    xp = x_prompt.reshape(batch * seq, d)
    xs = x_sample.reshape(db * t_new, d)
    pconv, sconv = [], []
    p_stacked = s_stacked = None
    for l in range(depth):
        p = _in_projection(xp, g_pre_mix[l], w_split, l, depth, p_stacked, tm=1024)
        p_stacked = (p["k"], p["v"], p["ik"])
        vt = p["vb"].reshape(batch, seq, aw).transpose(0, 2, 1)
        attn = _dsa_prompt(p["q"], p["iq"], p["iw"], p["ikb"], p["kb"], vt, batch, seq)
        y, pbuf = _conv_prompt(p["glu"], w_conv_dw[l], b_conv_dw[l], g_conv_ln[l], b_conv_ln[l], batch, seq)
        xp = _merge(attn, y, p["gates"], xp, wa, wc, wo, l, g_post_mix[l])
        xp = _ffn(xp, g_pre_ffn[l], wu, wd, l, g_post_ffn[l])
        pconv.append(pbuf)

        s = _in_projection(xs, g_pre_mix[l], w_split, l, depth, s_stacked, tm=db * t_new)
        s_stacked = (s["k"], s["v"], s["ik"])
        iq64 = s["iq"].reshape(db * t_new * IDX_HEADS, IDX_DIM)
        iwcol = s["iw"][:, :IDX_HEADS].reshape(db * t_new * IDX_HEADS, 1)
        ikn_pad = pad_rows(s["ikb"], PAGE_SIZE)
        scores = _s_scores(page_table, iq64, iwcol, cache_idx_k, l, ikn_pad, t_new)
        width = scores.shape[2]
        sel = _s_select(scores.reshape(db * SUBLANES, width), k_sel_s).reshape(db, SUBLANES, width)
        kn8 = pad_rows(s["k"][l], SUBLANES).reshape(db, SUBLANES, N_HEADS, HEAD_DIM)
        vn8 = pad_rows(s["v"][l], SUBLANES).reshape(db, SUBLANES, N_HEADS, HEAD_DIM)
        attn_s = _s_attn(page_table, pad_rows(s["q"], SUBLANES), sel, cache_k, cache_v, l, kn8, vn8)
        attn_s = attn_s[:, :t_new].reshape(db * t_new, aw)
        ys, sbuf = _conv_sample(s["glu"].reshape(db, t_new, 2 * ch), state_conv, l, w_conv_dw[l], b_conv_dw[l],
                                g_conv_ln[l], b_conv_ln[l])
        xs = _merge(attn_s, ys.reshape(db * t_new, ch), s["gates"], xs, wa, wc, wo, l, g_post_mix[l])
        xs = _ffn(xs, g_pre_ffn[l], wu, wd, l, g_post_ffn[l], tm=db * t_new)
        sconv.append(sbuf)

    pk, pv, pik = p_stacked
    sk, sv, sik = s_stacked
    return (xp.reshape(batch, seq, d), xs.reshape(db, t_new, d),
            pk.reshape(depth, batch, seq, N_HEADS, HEAD_DIM), pv.reshape(depth, batch, seq, N_HEADS, HEAD_DIM),
            pik.reshape(depth, batch, seq, IDX_DIM), jnp.stack(pconv),
            sk.reshape(depth, db, t_new, N_HEADS, HEAD_DIM), sv.reshape(depth, db, t_new, N_HEADS, HEAD_DIM),
            sik.reshape(depth, db, t_new, IDX_DIM), jnp.stack(sconv))
```

```python
import functools

import jax
import jax.numpy as jnp
import numpy as np
from jax import lax
from jax.experimental import pallas as pl
from jax.experimental.pallas import tpu as pltpu

N_HEADS = 8
HEAD_DIM = 128
IDX_HEADS = 16
IDX_DIM = 128
TOPK_MAX = 256
CONV_WIDTH = 31
CONV_BUF = CONV_WIDTH - 1
PAGE_SIZE = 128
NORM_EPS = 1e-6
LN_EPS = 1e-5

LANES = 128
SUBLANES = 8
VMEM_LIMIT = 56 * 1024 * 1024

INT_MIN = np.int32(-2 ** 31)
BF16 = jnp.bfloat16
F32 = jnp.float32


def _cparams(*sem):
    return pltpu.CompilerParams(dimension_semantics=sem, vmem_limit_bytes=VMEM_LIMIT)


def _rms(x, g):
    ms = jnp.mean(x * x, axis=-1, keepdims=True)
    return x * lax.rsqrt(ms + NORM_EPS) * g


def _nt_dot(a, b):
    return lax.dot_general(a, b, (((1,), (1,)), ((), ())), preferred_element_type=F32)


def _nn_dot(a, b):
    return jnp.dot(a, b, preferred_element_type=F32)


def _fold_rows(op, x, ways=4):
    parts = [x[r:r + SUBLANES] for r in range(0, x.shape[0], SUBLANES)]
    accs = parts[:ways]
    for i, part in enumerate(parts[ways:]):
        accs[i % ways] = op(accs[i % ways], part)
    while len(accs) > 1:
        accs = [op(a, b) for a, b in zip(accs[0::2], accs[1::2])] + (accs[-1:] if len(accs) % 2 else [])
    return accs[0]


def _order_key(x):
    bits = pltpu.bitcast(x, jnp.int32)
    return bits ^ ((bits >> 31) & jnp.int32(0x7FFFFFFF))


_TN = 512
_SEGS = (("q", 2), ("k", 2), ("v", 2), ("iq", 4), ("glu", 4), ("gates", 8), ("small", 1))
_SEG_START = {}
_acc = 0
for _n, _c in _SEGS:
    _SEG_START[_n] = _acc
    _acc += _c
_N_COL_TILES = _acc
_SEG_COUNT = dict(_SEGS)


def _split_w_in(w_in):
    aw = N_HEADS * HEAD_DIM
    ik0 = 3 * aw + IDX_HEADS * IDX_DIM
    iw0 = ik0 + IDX_DIM
    head = w_in[:, :, :iw0].astype(BF16)
    tail = w_in[:, :, iw0 + IDX_HEADS:].astype(BF16)
    iw = jnp.pad(w_in[:, :, iw0:iw0 + IDX_HEADS], ((0, 0), (0, 0), (0, LANES - IDX_HEADS))).astype(BF16)
    return head, tail, iw


def _inproj_kernel(x_ref, g_ref, wh_ref, wik_ref, wt_ref, wiw_ref, *rest, n_alias):
    (q_ref, k_ref, kb_ref, v_ref, vb_ref, iq_ref, glu_ref, gates_ref, ik_ref, ikb_ref, iw_ref,
     h_scr) = rest[n_alias:]
    j = pl.program_id(1)

    @pl.when(j == 0)
    def _():
        h_scr[...] = _rms(x_ref[...], g_ref[...]).astype(BF16)

    def seg(name):
        s = _SEG_START[name]
        return (j >= s) & (j < s + _SEG_COUNT[name])

    def proj(w_ref):
        return _nn_dot(h_scr[...], w_ref[...])

    @pl.when(seg("q"))
    def _():
        q_ref[...] = proj(wh_ref).astype(BF16)

    @pl.when(seg("k"))
    def _():
        r = proj(wh_ref)
        k_ref[...] = r
        kb_ref[...] = r.astype(BF16)

    @pl.when(seg("v"))
    def _():
        r = proj(wh_ref)
        v_ref[...] = r
        vb_ref[...] = r.astype(BF16)

    @pl.when(seg("iq"))
    def _():
        iq_ref[...] = proj(wh_ref).astype(BF16)

    @pl.when(seg("glu"))
    def _():
        glu_ref[...] = proj(wt_ref).astype(BF16)

    @pl.when(seg("gates"))
    def _():
        gates_ref[...] = proj(wt_ref).astype(BF16)

    @pl.when(seg("small"))
    def _():
        r = proj(wik_ref)
        ik_ref[...] = r
        ikb_ref[...] = r.astype(BF16)
        iw_ref[...] = proj(wiw_ref)


def _in_projection(x2d, g, w_split, layer, depth, stacked_prev, tm):
    m, d = x2d.shape
    aw = N_HEADS * HEAD_DIM
    w_head, w_tail, w_iw = w_split
    head_tiles = _SEG_START["glu"]
    tail_tiles = _SEG_COUNT["glu"] + _SEG_COUNT["gates"]

    def col_idx(name):
        s, c = _SEG_START[name], _SEG_COUNT[name]
        return lambda j: jnp.clip(j - s, 0, c - 1)

    def out(name, width_total, dtype, block_w=_TN):
        ci = col_idx(name)
        return (jax.ShapeDtypeStruct((m, width_total), dtype),
                pl.BlockSpec((tm, block_w), lambda i, j: (i, ci(j))))

    def out_stacked(name, width_total, block_w=_TN):
        ci = col_idx(name)
        return (jax.ShapeDtypeStruct((depth, m, width_total), F32),
                pl.BlockSpec((None, tm, block_w), lambda i, j: (layer, i, ci(j))))

    outs = [
        out("q", aw, BF16), out_stacked("k", aw), out("k", aw, BF16), out_stacked("v", aw), out("v", aw, BF16),
        out("iq", IDX_HEADS * IDX_DIM, BF16), out("glu", d, BF16), out("gates", 2 * d, BF16),
        out_stacked("small", IDX_DIM, IDX_DIM), out("small", IDX_DIM, BF16, IDX_DIM),
        out("small", LANES, F32, LANES),
    ]
    stacked_out_idx = (1, 3, 8)
    n_alias = 0 if stacked_prev is None else len(stacked_prev)
    n_in = 6
    aliases = {n_in + a: stacked_out_idx[a] for a in range(n_alias)}
    res = pl.pallas_call(
        functools.partial(_inproj_kernel, n_alias=n_alias),
        grid=(m // tm, _N_COL_TILES),
        in_specs=[pl.BlockSpec((tm, d), lambda i, j: (i, 0), pipeline_mode=pl.Buffered(1)),
                  pl.BlockSpec((1, d), lambda i, j: (0, 0)),
                  pl.BlockSpec((None, d, _TN), lambda i, j: (layer, 0, jnp.clip(j, 0, head_tiles - 1))),
                  pl.BlockSpec((None, d, IDX_DIM), lambda i, j: (layer, 0, head_tiles * _TN // IDX_DIM)),
                  pl.BlockSpec((None, d, _TN), lambda i, j: (layer, 0, jnp.clip(j - head_tiles, 0, tail_tiles - 1))),
                  pl.BlockSpec((None, d, LANES), lambda i, j: (layer, 0, 0))]
                 + [pl.BlockSpec(memory_space=pl.ANY)] * n_alias,
        out_specs=[o[1] for o in outs],
        out_shape=[o[0] for o in outs],
        input_output_aliases=aliases,
        scratch_shapes=[pltpu.VMEM((tm, d), BF16)],
        compiler_params=_cparams("parallel", "arbitrary"),
        name="in_projection",
    )(x2d, g.reshape(1, d), w_head, w_head, w_tail, w_iw, *(stacked_prev or ()))
    names = ("q", "k", "kb", "v", "vb", "iq", "glu", "gates", "ik", "ikb", "iw")
    return dict(zip(names, res))


_TAIL = 32


def _ln_swish(y, g, b):
    mu = jnp.mean(y, axis=-1, keepdims=True)
    yc = y - mu
    var = jnp.mean(yc * yc, axis=-1, keepdims=True)
    z = yc * lax.rsqrt(var + LN_EPS) * g + b
    return z * jax.nn.sigmoid(z)


def _conv_prompt_kernel(glu_ref, w_ref, bdw_ref, lng_ref, lnb_ref, y_ref, buf_ref, ext_scr, shift_scr, acc_scr, *,
                        tt, ch):
    i = pl.program_id(1)

    @pl.when(i == 0)
    def _():
        ext_scr[0:_TAIL, :] = jnp.zeros((_TAIL, ch), F32)

    a = glu_ref[:, :ch].astype(F32)
    g = glu_ref[:, ch:].astype(F32)
    ext_scr[_TAIL:, :] = a * jax.nn.sigmoid(g)

    off0 = _TAIL - CONV_BUF
    rows = 64
    span = tt + _TAIL - SUBLANES
    for r in range(1, SUBLANES):
        shift_scr[r, 0:span, :] = ext_scr[r:r + span, :]

    def shifted(r, start):
        return ext_scr[start:start + rows, cs] if r == 0 else shift_scr[r, start:start + rows, cs]

    for c in range(ch // LANES):
        cs = slice(c * LANES, (c + 1) * LANES)
        wc = w_ref[:, cs]
        for rc in range(tt // rows):
            acc = jnp.zeros((rows, LANES), F32)
            for j in range(CONV_WIDTH):
                a, r = divmod(off0 + j, SUBLANES)
                acc = acc + shifted(r, rc * rows + SUBLANES * a) * wc[j:j + 1, :]
            acc_scr[rc * rows:(rc + 1) * rows, cs] = acc

    y = acc_scr[...] + bdw_ref[...]
    y_ref[...] = _ln_swish(y, lng_ref[...], lnb_ref[...]).astype(y_ref.dtype)

    @pl.when(i == pl.num_programs(1) - 1)
    def _():
        buf_ref[...] = ext_scr[tt + _TAIL - CONV_BUF: tt + _TAIL, :]

    ext_scr[0:_TAIL, :] = ext_scr[tt:tt + _TAIL, :]


def _conv_prompt(glu2d, w_dw, b_dw, ln_g, ln_b, batch, seq, tt=256):
    ch = w_dw.shape[1]
    nt = seq // tt
    row = lambda a: a.reshape(1, ch)
    vec = pl.BlockSpec((1, ch), lambda b, i: (0, 0))
    y, buf = pl.pallas_call(
        functools.partial(_conv_prompt_kernel, tt=tt, ch=ch),
        grid=(batch, nt),
        in_specs=[pl.BlockSpec((tt, 2 * ch), lambda b, i: (b * nt + i, 0)),
                  pl.BlockSpec((CONV_WIDTH, ch), lambda b, i: (0, 0)), vec, vec, vec],
        out_specs=[pl.BlockSpec((tt, ch), lambda b, i: (b * nt + i, 0)),
                   pl.BlockSpec((None, CONV_BUF, ch), lambda b, i: (b, 0, 0))],
        out_shape=[jax.ShapeDtypeStruct((batch * seq, ch), BF16),
                   jax.ShapeDtypeStruct((batch, CONV_BUF, ch), F32)],
        scratch_shapes=[pltpu.VMEM((tt + _TAIL, ch), F32), pltpu.VMEM((SUBLANES, tt + _TAIL, ch), F32),
                        pltpu.VMEM((tt, ch), F32)],
        compiler_params=_cparams("parallel", "arbitrary"),
        name="conv_prompt",
    )(glu2d, w_dw, row(b_dw), row(ln_g), row(ln_b))
    return y, buf


def _conv_sample_kernel(glu_ref, st_ref, w_ref, bdw_ref, lng_ref, lnb_ref, y_ref, buf_ref, full_scr, *, t_new, ch):
    g_sz = st_ref.shape[0]
    a = glu_ref[:, :, :ch].astype(F32)
    g = glu_ref[:, :, ch:].astype(F32)
    full_scr[:, 0:CONV_BUF, :] = st_ref[...]
    full_scr[:, CONV_BUF:CONV_BUF + t_new, :] = a * jax.nn.sigmoid(g)
    pad0 = CONV_BUF + t_new
    full_scr[:, pad0:, :] = jnp.zeros((g_sz, full_scr.shape[1] - pad0, ch), F32)
    w = w_ref[...]
    for t in range(t_new):
        y = jnp.sum(full_scr[:, t:t + CONV_WIDTH + 1, :] * w[None], axis=1) + bdw_ref[...]
        y_ref[:, t, :] = _ln_swish(y, lng_ref[...], lnb_ref[...]).astype(y_ref.dtype)
    buf_ref[...] = full_scr[:, t_new:t_new + CONV_BUF, :]


def _conv_sample(glu3d, state_all, layer, w_dw, b_dw, ln_g, ln_b, g_sz=16):
    db, t_new, _ = glu3d.shape
    ch = w_dw.shape[1]
    row = lambda a: a.reshape(1, ch)
    vec = pl.BlockSpec((1, ch), lambda b: (0, 0))
    w_pad = jnp.concatenate([w_dw, jnp.zeros((1, ch), w_dw.dtype)], axis=0)
    y, buf = pl.pallas_call(
        functools.partial(_conv_sample_kernel, t_new=t_new, ch=ch),
        grid=(db // g_sz,),
        in_specs=[pl.BlockSpec((g_sz, t_new, 2 * ch), lambda b: (b, 0, 0)),
                  pl.BlockSpec((None, g_sz, CONV_BUF, ch), lambda b: (layer, b, 0, 0)),
                  pl.BlockSpec((CONV_WIDTH + 1, ch), lambda b: (0, 0)), vec, vec, vec],
        out_specs=[pl.BlockSpec((g_sz, t_new, ch), lambda b: (b, 0, 0)),
                   pl.BlockSpec((g_sz, CONV_BUF, ch), lambda b: (b, 0, 0))],
        out_shape=[jax.ShapeDtypeStruct((db, t_new, ch), BF16),
                   jax.ShapeDtypeStruct((db, CONV_BUF, ch), F32)],
        scratch_shapes=[pltpu.VMEM((g_sz, 40, ch), F32)],
        compiler_params=_cparams("parallel"),
        name="conv_sample",
    )(glu3d, state_all, w_pad, row(b_dw), row(ln_g), row(ln_b))
    return y, buf


def _topk_threshold(count_ge, shape, k_sel):
    def bit_step(b, r):
        trial = r + lax.shift_left(jnp.int32(1), jnp.int32(31) - b)
        return jnp.where(count_ge(trial) >= k_sel, trial, r)
    r = lax.fori_loop(0, 32, bit_step, jnp.full(shape, INT_MIN, jnp.int32))
    return jnp.maximum(r, INT_MIN + 1)


def _dsa_prompt_kernel(q_ref, iq_ref, iw_ref, iqn_ref, iwn_ref, ik_ref, k_ref, vt_ref, o_ref,
                       key_scr, m_scr, l_scr, acc_scr, *, tq, k_sel):
    i = pl.program_id(1)
    nq = pl.num_programs(1)
    n_chunks = i + 1
    sub = LANES
    idx_scale = IDX_DIM ** -0.5 * IDX_HEADS ** -0.5
    cur = key_scr.at[i % 2]
    nxt = key_scr.at[(i + 1) % 2]

    def score_rows(dst, iq_r, w_rows, q_start, c, s_i):
        r0 = pl.multiple_of(c * tq + s_i * sub, sub)
        ikc = ik_ref[pl.ds(r0, sub), :]
        acc = jnp.zeros((sub, tq), F32)
        for h in range(IDX_HEADS):
            s = _nt_dot(ikc, iq_r[:, h * IDX_DIM:(h + 1) * IDX_DIM])
            acc = acc + jnp.maximum(s, 0.0) * w_rows[h:h + 1, :]
        kidx = r0 + lax.broadcasted_iota(jnp.int32, (sub, tq), 0)
        qidx = q_start + lax.broadcasted_iota(jnp.int32, (sub, tq), 1)
        dst[pl.ds(r0, sub), :] = jnp.where(kidx <= qidx, _order_key(acc), INT_MIN)

    @pl.when(i == 0)
    def _():
        w_rows = iw_ref[...].T * idx_scale
        for s_i in range(tq // sub):
            score_rows(cur, iq_ref, w_rows, 0, 0, s_i)

    @pl.when((n_chunks % 2 == 1) & (n_chunks < nq))
    def _():
        cur[pl.ds(pl.multiple_of(n_chunks * tq, tq), tq), :] = jnp.full((tq, tq), INT_MIN, jnp.int32)

    def count_ge(trial):
        def cnt_pair(c, cnt):
            r0 = pl.multiple_of(c * 2 * tq, 2 * tq)
            m = jnp.where(cur[pl.ds(r0, 2 * tq), :] >= trial, 1, 0)
            return cnt + _fold_rows(jnp.add, m)
        cnt = lax.fori_loop(0, (n_chunks + 1) // 2, cnt_pair, jnp.zeros((SUBLANES, tq), jnp.int32))
        return cnt.sum(axis=0, keepdims=True)

    thr = _topk_threshold(count_ge, (1, tq), k_sel)

    scale = HEAD_DIM ** -0.5 * float(np.log2(np.e))
    m_scr[...] = jnp.full(m_scr.shape, -jnp.inf, F32)
    l_scr[...] = jnp.zeros(l_scr.shape, F32)
    acc_scr[...] = jnp.zeros(acc_scr.shape, F32)
    head_cols = [slice(h * HEAD_DIM, (h + 1) * HEAD_DIM) for h in range(N_HEADS)]
    wn_rows = iwn_ref[...].T * idx_scale
    qn_start = (i + 1) * tq

    def attn_chunk(c, carry):
        r0 = pl.multiple_of(c * tq, tq)
        sel = cur[pl.ds(r0, tq), :] >= thr
        lgs = [_nt_dot(k_ref[pl.ds(r0, tq), hs], q_ref[:, hs]) for hs in head_cols]
        score_rows(nxt, iqn_ref, wn_rows, qn_start, c, 0)
        ps, alphas = [], []
        for h in range(N_HEADS):
            ss = slice(h * SUBLANES, (h + 1) * SUBLANES)
            lg = jnp.where(sel, lgs[h] * scale, -jnp.inf)
            mc = _fold_rows(jnp.maximum, lg).max(axis=0, keepdims=True)
            m_old = m_scr[ss, :]
            m_new = jnp.maximum(m_old, mc)
            m_safe = jnp.where(m_new == -jnp.inf, 0.0, m_new)
            alpha = jnp.exp2(m_old - m_safe)
            p = jnp.exp2(lg - m_safe[0:1, :])
            l_scr[ss, :] = alpha * l_scr[ss, :] + _fold_rows(jnp.add, p)
            m_scr[ss, :] = m_new
            ps.append(p.astype(BF16))
            alphas.append(alpha[0:1, :])
        for s_i in range(1, tq // sub):
            score_rows(nxt, iqn_ref, wn_rows, qn_start, c, s_i)
        for h, hs in enumerate(head_cols):
            pv = _nn_dot(vt_ref[hs, pl.ds(r0, tq)], ps[h])
            acc_scr[hs, :] = alphas[h] * acc_scr[hs, :] + pv
        return carry

    lax.fori_loop(0, n_chunks, attn_chunk, 0)

    @pl.when(i + 1 < nq)
    def _():
        for s_i in range(tq // sub):
            score_rows(nxt, iqn_ref, wn_rows, qn_start, i + 1, s_i)

    for h in range(N_HEADS):
        hs = slice(h * HEAD_DIM, (h + 1) * HEAD_DIM)
        l = l_scr[h * SUBLANES:(h + 1) * SUBLANES, :].sum(axis=0, keepdims=True)
        o_ref[:, hs] = (acc_scr[hs, :] * (1.0 / l)).T.astype(o_ref.dtype)


def _dsa_prompt(q, iq, iw, ikb, kb, vt, batch, seq, tq=256):
    nq = seq // tq
    aw = N_HEADS * HEAD_DIM
    k_sel = min(TOPK_MAX, seq // 4)
    blk = lambda w: pl.BlockSpec((tq, w), lambda b, i: (b * nq + i, 0))
    nxt = lambda w: pl.BlockSpec((tq, w), lambda b, i: (b * nq + jnp.minimum(i + 1, nq - 1), 0))
    return pl.pallas_call(
        functools.partial(_dsa_prompt_kernel, tq=tq, k_sel=k_sel),
        grid=(batch, nq),
        in_specs=[blk(aw), blk(IDX_HEADS * IDX_DIM), blk(LANES), nxt(IDX_HEADS * IDX_DIM), nxt(LANES),
                  pl.BlockSpec((seq, IDX_DIM), lambda b, i: (b, 0)),
                  pl.BlockSpec((seq, aw), lambda b, i: (b, 0)),
                  pl.BlockSpec((None, aw, seq), lambda b, i: (b, 0, 0))],
        out_specs=blk(aw),
        out_shape=jax.ShapeDtypeStruct((batch * seq, aw), BF16),
        scratch_shapes=[pltpu.VMEM((2, seq, tq), jnp.int32),
                        pltpu.VMEM((N_HEADS * SUBLANES, tq), F32), pltpu.VMEM((N_HEADS * SUBLANES, tq), F32),
                        pltpu.VMEM((aw, tq), F32)],
        compiler_params=_cparams("parallel", "arbitrary"),
        name="dsa_prompt",
    )(q, iq, iw, iq, iw, ikb, kb, vt)


def _s_scores_kernel(pt_ref, iq_ref, iwc_ref, *rest, n_pages, t_new):
    pages, ikn_ref, o_ref = rest[:n_pages], rest[n_pages], rest[n_pages + 1]
    iq = iq_ref[...]
    wc = iwc_ref[...] * (IDX_DIM ** -0.5 * IDX_HEADS ** -0.5)
    past = n_pages * PAGE_SIZE
    width = past + PAGE_SIZE
    ik_all = jnp.concatenate([pg[...].astype(BF16) for pg in pages] + [ikn_ref[...]], axis=0)
    z = jnp.maximum(_nt_dot(iq, ik_all), 0.0) * wc
    sc = z.reshape(t_new, IDX_HEADS, width).sum(axis=1)
    key_i = lax.broadcasted_iota(jnp.int32, sc.shape, 1)
    t_i = lax.broadcasted_iota(jnp.int32, sc.shape, 0)
    o_ref[...] = jnp.full(o_ref.shape, -jnp.inf, F32)
    o_ref[0:t_new, :] = jnp.where(key_i <= past + t_i, sc, -jnp.inf)


def _s_scores(page_table, iq64, iwcol, cache_ik, layer, ikn_pad, t_new):
    db, n_pages = page_table.shape
    rows = t_new * IDX_HEADS
    width = (n_pages + 1) * PAGE_SIZE

    def page_spec(j):
        return pl.BlockSpec((None, None, PAGE_SIZE, IDX_DIM), lambda b, pt: (layer, pt[b, j], 0, 0))

    grid_spec = pltpu.PrefetchScalarGridSpec(
        num_scalar_prefetch=1,
        grid=(db,),
        in_specs=[pl.BlockSpec((rows, IDX_DIM), lambda b, pt: (b, 0)),
                  pl.BlockSpec((rows, 1), lambda b, pt: (b, 0))]
                 + [page_spec(j) for j in range(n_pages)]
                 + [pl.BlockSpec((None, PAGE_SIZE, IDX_DIM), lambda b, pt: (b, 0, 0))],
        out_specs=pl.BlockSpec((None, SUBLANES, width), lambda b, pt: (b, 0, 0)),
    )
    return pl.pallas_call(
        functools.partial(_s_scores_kernel, n_pages=n_pages, t_new=t_new),
        grid_spec=grid_spec,
        out_shape=jax.ShapeDtypeStruct((db, SUBLANES, width), F32),
        compiler_params=_cparams("arbitrary"),
        name="sample_scores",
    )(page_table, iq64, iwcol, *([cache_ik] * n_pages), ikn_pad)


def _s_select_kernel(s_ref, o_ref, key_scr, *, k_sel):
    s = s_ref[...]
    key_scr[...] = jnp.where(s == -jnp.inf, INT_MIN, _order_key(s))

    def count_ge(trial):
        return jnp.sum(jnp.where(key_scr[...] >= trial, 1.0, 0.0), axis=1, keepdims=True).astype(jnp.int32)

    thr = _topk_threshold(count_ge, (s.shape[0], 1), k_sel)
    o_ref[...] = jnp.where(key_scr[...] >= thr, 1.0, 0.0)


def _s_select(scores2d, k_sel, rows=256):
    m, width = scores2d.shape
    return pl.pallas_call(
        functools.partial(_s_select_kernel, k_sel=k_sel),
        grid=(m // rows,),
        in_specs=[pl.BlockSpec((rows, width), lambda i: (i, 0))],
        out_specs=pl.BlockSpec((rows, width), lambda i: (i, 0)),
        out_shape=jax.ShapeDtypeStruct((m, width), F32),
        scratch_shapes=[pltpu.VMEM((rows, width), jnp.int32)],
        compiler_params=_cparams("parallel"),
        name="sample_select",
    )(scores2d)


def _s_attn_kernel(pt_ref, q_ref, sel_ref, *rest, n_pages):
    kpages, vpages = rest[:n_pages], rest[n_pages:2 * n_pages]
    kn_ref, vn_ref, o_ref = rest[2 * n_pages:]
    rows = N_HEADS * SUBLANES
    kh = PAGE_SIZE * N_HEADS
    scale = HEAD_DIM ** -0.5 * float(np.log2(np.e))

    qall = jnp.concatenate([q_ref[:, h * HEAD_DIM:(h + 1) * HEAD_DIM] for h in range(N_HEADS)], axis=0)
    lane_key = lax.broadcasted_iota(jnp.int32, (PAGE_SIZE, kh), 1) // N_HEADS
    expand = jnp.where(lane_key == lax.broadcasted_iota(jnp.int32, (PAGE_SIZE, kh), 0), 1.0, 0.0).astype(BF16)
    same_head = (lax.broadcasted_iota(jnp.int32, (rows, kh), 1) % N_HEADS
                 == lax.broadcasted_iota(jnp.int32, (rows, kh), 0) // SUBLANES)

    def flat(ref):
        return ref[...].reshape(-1, HEAD_DIM).astype(BF16)

    def masked_logits(kf, j, width):
        sel8 = sel_ref[:, j * PAGE_SIZE:(j + 1) * PAGE_SIZE].astype(BF16)
        picked = _nn_dot(jnp.concatenate([sel8] * N_HEADS, axis=0), expand[:, :width]) > 0.5
        return jnp.where(picked & same_head[:, :width], _nt_dot(qall, kf) * scale, -jnp.inf)

    zrows = jnp.zeros((LANES - N_HEADS * SUBLANES, HEAD_DIM), BF16)
    kn = jnp.concatenate([flat(kn_ref), zrows], axis=0)
    vn = jnp.concatenate([flat(vn_ref), zrows], axis=0)
    lgs = [masked_logits(flat(kpages[j]), j, kh) for j in range(n_pages)]
    lg_n = masked_logits(kn, n_pages, LANES)

    mx = lgs[0]
    for x in lgs[1:]:
        mx = jnp.maximum(mx, x)
    m = jnp.maximum(jnp.max(mx, axis=1, keepdims=True), jnp.max(lg_n, axis=1, keepdims=True))
    m = jnp.where(m == -jnp.inf, 0.0, m)
    p_n = jnp.exp2(lg_n - m)
    l = jnp.sum(p_n, axis=1, keepdims=True)
    acc = _nn_dot(p_n.astype(BF16), vn)
    for j in range(n_pages):
        p = jnp.exp2(lgs[j] - m)
        l = l + jnp.sum(p, axis=1, keepdims=True)
        acc = acc + _nn_dot(p.astype(BF16), flat(vpages[j]))
    out = acc / jnp.where(l == 0.0, 1.0, l)
    for h in range(N_HEADS):
        o_ref[:, h * HEAD_DIM:(h + 1) * HEAD_DIM] = out[h * SUBLANES:(h + 1) * SUBLANES, :].astype(o_ref.dtype)


def _s_attn(page_table, q8, sel, cache_k, cache_v, layer, kn8, vn8):
    db, n_pages = page_table.shape
    aw = N_HEADS * HEAD_DIM
    width = sel.shape[2]
    row_blk = pl.BlockSpec((None, SUBLANES, aw), lambda b, pt: (b, 0, 0))
    new_blk = pl.BlockSpec((None, SUBLANES, N_HEADS, HEAD_DIM), lambda b, pt: (b, 0, 0, 0))

    def page_blk(j):
        return pl.BlockSpec((None, None, PAGE_SIZE, N_HEADS, HEAD_DIM), lambda b, pt: (layer, pt[b, j], 0, 0, 0))

    pages = [page_blk(j) for j in range(n_pages)]
    grid_spec = pltpu.PrefetchScalarGridSpec(
        num_scalar_prefetch=1,
        grid=(db,),
        in_specs=[row_blk, pl.BlockSpec((None, SUBLANES, width), lambda b, pt: (b, 0, 0))]
                 + pages + pages + [new_blk, new_blk],
        out_specs=row_blk,
    )
    return pl.pallas_call(
        functools.partial(_s_attn_kernel, n_pages=n_pages),
        grid_spec=grid_spec,
        out_shape=jax.ShapeDtypeStruct((db, SUBLANES, aw), BF16),
        compiler_params=_cparams("arbitrary"),
        name="sample_attention",
    )(page_table, q8, sel, *([cache_k] * n_pages), *([cache_v] * n_pages), kn8, vn8)


def _merge_kernel(attn_ref, y_ref, gates_ref, x_ref, wa_ref, wc_ref, wo_ref, g_ref, o_ref):
    d = x_ref.shape[1]
    a = _nn_dot(attn_ref[...], wa_ref[...])
    c = _nn_dot(y_ref[...], wc_ref[...])
    g_a = jax.nn.sigmoid(gates_ref[:, :d].astype(F32))
    g_c = jax.nn.sigmoid(gates_ref[:, d:].astype(F32))
    mixed = g_a * a + g_c * c
    o = _nn_dot(mixed.astype(BF16), wo_ref[...])
    o_ref[...] = x_ref[...] + _rms(o, g_ref[...])


def _merge(attn, y, gates, x2d, wa, wc, wo, layer, g, tm=256):
    m, d = x2d.shape
    aw, ch = attn.shape[1], y.shape[1]
    w_full = lambda r, c: pl.BlockSpec((None, r, c), lambda i: (layer, 0, 0))
    blk = lambda w: pl.BlockSpec((tm, w), lambda i: (i, 0))
    return pl.pallas_call(
        _merge_kernel,
        grid=(m // tm,),
        in_specs=[blk(aw), blk(ch), blk(2 * d), blk(d), w_full(aw, d), w_full(ch, d), w_full(d, d),
                  pl.BlockSpec((1, d), lambda i: (0, 0))],
        out_specs=blk(d),
        out_shape=jax.ShapeDtypeStruct((m, d), F32),
        compiler_params=_cparams("parallel"),
        name="merge",
    )(attn, y, gates, x2d, wa, wc, wo, g.reshape(1, d))


def _ffn_kernel(x_ref, gpre_ref, wu_ref, wd_ref, gpost_ref, o_ref, f_scr, acc_scr):
    f = pl.program_id(1)

    @pl.when(f == 0)
    def _():
        f_scr[...] = _rms(x_ref[...], gpre_ref[...]).astype(BF16)
        acc_scr[...] = jnp.zeros(acc_scr.shape, F32)

    hdn = jnp.square(jnp.maximum(_nn_dot(f_scr[...], wu_ref[...]), 0.0)).astype(BF16)
    acc_scr[...] += _nn_dot(hdn, wd_ref[...])

    @pl.when(f == pl.num_programs(1) - 1)
    def _():
        o_ref[...] = x_ref[...] + _rms(acc_scr[...], gpost_ref[...])


def _ffn(x2d, g_pre, w_up, w_down, layer, g_post, tm=512, tf=512):
    m, d = x2d.shape
    d_ff = w_up.shape[2]
    vec = pl.BlockSpec((1, d), lambda i, f: (0, 0))
    return pl.pallas_call(
        _ffn_kernel,
        grid=(m // tm, d_ff // tf),
        in_specs=[pl.BlockSpec((tm, d), lambda i, f: (i, 0)), vec,
                  pl.BlockSpec((None, d, tf), lambda i, f: (layer, 0, f)),
                  pl.BlockSpec((None, tf, d), lambda i, f: (layer, f, 0)), vec],
        out_specs=pl.BlockSpec((tm, d), lambda i, f: (i, 0)),
        out_shape=jax.ShapeDtypeStruct((m, d), F32),
        scratch_shapes=[pltpu.VMEM((tm, d), BF16), pltpu.VMEM((tm, d), F32)],
        compiler_params=_cparams("parallel", "arbitrary"),
        name="ffn",
    )(x2d, g_pre.reshape(1, d), w_up, w_down, g_post.reshape(1, d))


def kernel(x_prompt, x_sample, cache_k, cache_v, cache_idx_k, state_conv, page_table, w_in, w_attn_up, w_conv_dw,
           b_conv_dw, g_conv_ln, b_conv_ln, w_conv_out, w_o, w_ffn_up, w_ffn_down, g_pre_mix, g_post_mix,
           g_pre_ffn, g_post_ffn):
    depth = w_in.shape[0]
    batch, seq, d = x_prompt.shape
    db, t_new, _ = x_sample.shape
    n_pages = page_table.shape[1]
    aw = N_HEADS * HEAD_DIM
    ch = d // 2
    k_sel_s = min(TOPK_MAX, (n_pages * PAGE_SIZE + t_new) // 4)

    def pad_rows(a2d, rows):
        a3 = a2d.reshape(db, t_new, a2d.shape[1])
        return jnp.pad(a3, ((0, 0), (0, rows - t_new), (0, 0)))

    w_split = _split_w_in(w_in)
    wa, wc, wo = w_attn_up.astype(BF16), w_conv_out.astype(BF16), w_o.astype(BF16)
    wu, wd = w_ffn_up.astype(BF16), w_ffn_down.astype(BF16)

    xp = x_prompt.reshape(batch * seq, d)
    xs = x_sample.reshape(db * t_new, d)
    pconv, sconv = [], []
    p_stacked = s_stacked = None
    for l in range(depth):
        p = _in_projection(xp, g_pre_mix[l], w_split, l, depth, p_stacked, tm=1024)
        p_stacked = (p["k"], p["v"], p["ik"])
        vt = p["vb"].reshape(batch, seq, aw).transpose(0, 2, 1)
        attn = _dsa_prompt(p["q"], p["iq"], p["iw"], p["ikb"], p["kb"], vt, batch, seq)
        y, pbuf = _conv_prompt(p["glu"], w_conv_dw[l], b_conv_dw[l], g_conv_ln[l], b_conv_ln[l], batch, seq)
        xp = _merge(attn, y, p["gates"], xp, wa, wc, wo, l, g_post_mix[l])
        xp = _ffn(xp, g_pre_ffn[l], wu, wd, l, g_post_ffn[l])
        pconv.append(pbuf)

        s = _in_projection(xs, g_pre_mix[l], w_split, l, depth, s_stacked, tm=db * t_new)
        s_stacked = (s["k"], s["v"], s["ik"])
        iq64 = s["iq"].reshape(db * t_new * IDX_HEADS, IDX_DIM)
        iwcol = s["iw"][:, :IDX_HEADS].reshape(db * t_new * IDX_HEADS, 1)
        ikn_pad = pad_rows(s["ikb"], PAGE_SIZE)
        scores = _s_scores(page_table, iq64, iwcol, cache_idx_k, l, ikn_pad, t_new)
        width = scores.shape[2]
        sel = _s_select(scores.reshape(db * SUBLANES, width), k_sel_s).reshape(db, SUBLANES, width)
        kn8 = pad_rows(s["k"][l], SUBLANES).reshape(db, SUBLANES, N_HEADS, HEAD_DIM)
        vn8 = pad_rows(s["v"][l], SUBLANES).reshape(db, SUBLANES, N_HEADS, HEAD_DIM)
        attn_s = _s_attn(page_table, pad_rows(s["q"], SUBLANES), sel, cache_k, cache_v, l, kn8, vn8)
        attn_s = attn_s[:, :t_new].reshape(db * t_new, aw)
        ys, sbuf = _conv_sample(s["glu"].reshape(db, t_new, 2 * ch), state_conv, l, w_conv_dw[l], b_conv_dw[l],
                                g_conv_ln[l], b_conv_ln[l])
        xs = _merge(attn_s, ys.reshape(db * t_new, ch), s["gates"], xs, wa, wc, wo, l, g_post_mix[l])
        xs = _ffn(xs, g_pre_ffn[l], wu, wd, l, g_post_ffn[l], tm=db * t_new)
        sconv.append(sbuf)

    pk, pv, pik = p_stacked
    sk, sv, sik = s_stacked
    return (xp.reshape(batch, seq, d), xs.reshape(db, t_new, d),
            pk.reshape(depth, batch, seq, N_HEADS, HEAD_DIM), pv.reshape(depth, batch, seq, N_HEADS, HEAD_DIM),
            pik.reshape(depth, batch, seq, IDX_DIM), jnp.stack(pconv),
            sk.reshape(depth, db, t_new, N_HEADS, HEAD_DIM), sv.reshape(depth, db, t_new, N_HEADS, HEAD_DIM),
            sik.reshape(depth, db, t_new, IDX_DIM), jnp.stack(sconv))
```
